```python
import jax, jax.numpy as jnp
from jax import lax
import numpy as np

D_MODEL = 1024
BATCH = 1
SEQ = 16384
DEPTH = 1
DEC_BATCH = 8
DEC_SEQ = 64
PAST_LEN = 1024

CHUNK = 64
D_SSM = D_MODEL
SSM_GROUP = 16
N_GROUPS = D_SSM // SSM_GROUP
SSM_STATE = 64
D_CONV = D_MODEL
CONV_WIDTH = 31
D_FF = 2816
N_MOD = 9
D_IN = D_SSM + 2 * D_CONV + 2 * D_MODEL
EPS = 1e-6
DT_MIN = 1e-3
DT_MAX = 1e-1

kernel_name = "gated_s5_conformer_conv_streaming_encoder"


def _rmsnorm(x, g):
    xf = x.astype(jnp.float32)
    y = xf * lax.rsqrt(jnp.mean(xf * xf, axis=-1, keepdims=True) + EPS) * g.astype(jnp.float32)
    return y.astype(x.dtype)


def _layernorm(x, g, b):
    xf = x.astype(jnp.float32)
    mu = jnp.mean(xf, axis=-1, keepdims=True)
    var = jnp.mean(jnp.square(xf - mu), axis=-1, keepdims=True)
    y = (xf - mu) * lax.rsqrt(var + EPS) * g.astype(jnp.float32) + b.astype(jnp.float32)
    return y.astype(x.dtype)


def _modulate(x, shift, scale):
    return x * (1.0 + scale[:, None, :]) + shift[:, None, :]


def _swiglu(x, w1, w2):
    gu = x @ w1
    g, u = jnp.split(gu, 2, axis=-1)
    return (jax.nn.silu(g) * u) @ w2


def _scan_combine(e1, e2):
    a1r, a1i, b1r, b1i = e1
    a2r, a2i, b2r, b2i = e2
    return (a2r * a1r - a2i * a1i,
            a2r * a1i + a2i * a1r,
            a2r * b1r - a2i * b1i + b2r,
            a2r * b1i + a2i * b1r + b2i)


def _s5(u, x0r, x0i, lam_re, lam_im, log_dt, b_re, b_im, c_re, c_im, d_skip):
    bsz, slen = u.shape[0], u.shape[1]
    uf = u.astype(jnp.float32).reshape(bsz, slen, N_GROUPS, SSM_GROUP)
    lr = lam_re.astype(jnp.float32)
    li = lam_im.astype(jnp.float32)
    dt = jnp.exp(log_dt.astype(jnp.float32))[:, None]
    mag = jnp.exp(lr * dt)
    ar = mag * jnp.cos(li * dt)
    ai = mag * jnp.sin(li * dt)
    den = lr * lr + li * li
    kr = ((ar - 1.0) * lr + ai * li) / den
    ki = (ai * lr - (ar - 1.0) * li) / den
    br = b_re.astype(jnp.float32)
    bi = b_im.astype(jnp.float32)
    bbr = kr[..., None] * br - ki[..., None] * bi
    bbi = kr[..., None] * bi + ki[..., None] * br
    bur = jnp.einsum('gpm,blgm->blgp', bbr, uf)
    bui = jnp.einsum('gpm,blgm->blgp', bbi, uf)
    x0r = x0r.astype(jnp.float32)
    x0i = x0i.astype(jnp.float32)
    bur = bur.at[:, 0].add(ar * x0r - ai * x0i)
    bui = bui.at[:, 0].add(ar * x0i + ai * x0r)
    a_r = jnp.broadcast_to(ar, bur.shape)
    a_i = jnp.broadcast_to(ai, bur.shape)
    _, _, xr, xi = lax.associative_scan(_scan_combine, (a_r, a_i, bur, bui), axis=1)
    y = (jnp.einsum('gmp,blgp->blgm', c_re.astype(jnp.float32), xr)
         - jnp.einsum('gmp,blgp->blgm', c_im.astype(jnp.float32), xi)
         + d_skip.astype(jnp.float32) * uf)
    return y.reshape(bsz, slen, D_SSM).astype(u.dtype), xr[:, -1], xi[:, -1]


def _causal_dwconv(v, buf, w_dw, b_dw):
    xp = jnp.concatenate([buf.astype(v.dtype), v], axis=1)
    y = lax.conv_general_dilated(xp, w_dw[:, None, :].astype(v.dtype), window_strides=(1,), padding='VALID',
                                 dimension_numbers=('NWC', 'WIO', 'NWC'), feature_group_count=D_CONV)
    return y + b_dw, xp[:, -(CONV_WIDTH - 1):]


def _layer(h, c, s_re, s_im, conv_buf, w_ada, b_ada, g_ffn1, w1_ffn1, w2_ffn1, g_mix, w_in, b_in,
           lam_re, lam_im, log_dt, b_re, b_im, c_re, c_im, d_skip, w_glu, w_dw, b_dw, ln_g, ln_b,
           w_pw, b_pw, w_out, g_ffn2, w1_ffn2, w2_ffn2):
    mod = (jax.nn.silu(c) @ w_ada + b_ada).reshape(c.shape[0], N_MOD, D_MODEL)
    sh1, sc1, gt1 = mod[:, 0], mod[:, 1], mod[:, 2]
    sh2, sc2, gt2 = mod[:, 3], mod[:, 4], mod[:, 5]
    sh3, sc3, gt3 = mod[:, 6], mod[:, 7], mod[:, 8]
    h = h + 0.5 * gt1[:, None, :] * _swiglu(_modulate(_rmsnorm(h, g_ffn1), sh1, sc1), w1_ffn1, w2_ffn1)
    n = _modulate(_rmsnorm(h, g_mix), sh2, sc2)
    proj = n @ w_in + b_in
    u_s = proj[..., :D_SSM]
    v_c = proj[..., D_SSM:D_SSM + 2 * D_CONV]
    g_logit = proj[..., D_SSM + 2 * D_CONV:]
    y_s, new_re, new_im = _s5(u_s, s_re, s_im, lam_re, lam_im, log_dt, b_re, b_im, c_re, c_im, d_skip)
    ya_val, ya_gate = jnp.split(jax.nn.gelu(y_s) @ w_glu, 2, axis=-1)
    y_a = ya_val * jax.nn.sigmoid(ya_gate)
    cv_val, cv_gate = jnp.split(v_c, 2, axis=-1)
    v = cv_val * jax.nn.sigmoid(cv_gate)
    v, new_buf = _causal_dwconv(v, conv_buf, w_dw, b_dw)
    v = jax.nn.silu(_layernorm(v, ln_g, ln_b))
    y_b = v @ w_pw + b_pw
    gates = jax.nn.sigmoid(g_logit)
    merged = gates[..., :D_MODEL] * y_a + gates[..., D_MODEL:] * y_b
    h = h + gt2[:, None, :] * (merged @ w_out)
    h = h + 0.5 * gt3[:, None, :] * _swiglu(_modulate(_rmsnorm(h, g_ffn2), sh3, sc3), w1_ffn2, w2_ffn2)
    return h, new_re, new_im, new_buf


def _trunk(x, c, s_re, s_im, conv_buf, layer_params, g_final, w_ada_f, b_ada_f):
    h = x
    out_re, out_im, out_buf = [], [], []
    for l in range(DEPTH):
        p_l = tuple(w[l] for w in layer_params)
        h, nr, ni, nb = _layer(h, c, s_re[l], s_im[l], conv_buf[l], *p_l)
        out_re.append(nr)
        out_im.append(ni)
        out_buf.append(nb)
    modf = jax.nn.silu(c) @ w_ada_f + b_ada_f
    y = _modulate(_rmsnorm(h, g_final), modf[:, :D_MODEL], modf[:, D_MODEL:])
    return y, jnp.stack(out_re), jnp.stack(out_im), jnp.stack(out_buf)


def setup_inputs(seed: int = 0) -> dict:
    key = jax.random.key(seed)
    ks = iter(jax.random.split(key, 64))
    f32 = jnp.float32

    def nrm(shape, scale):
        return jax.random.normal(next(ks), shape, f32) * scale

    def gain(shape):
        return 1.0 + 0.02 * jax.random.normal(next(ks), shape, f32)

    L, D = DEPTH, D_MODEL
    G, P, M = N_GROUPS, SSM_STATE, SSM_GROUP
    n_idx = jnp.arange(P, dtype=f32)
    inp = {}
    inp['x_prompt'] = nrm((BATCH, SEQ, D), 1.0)
    inp['x_sample'] = nrm((DEC_BATCH, DEC_SEQ, D), 1.0)
    inp['c_prompt'] = nrm((BATCH, D), 1.0)
    inp['c_sample'] = nrm((DEC_BATCH, D), 1.0)
    inp['state_ssm_re'] = nrm((L, DEC_BATCH, G, P), 0.1)
    inp['state_ssm_im'] = nrm((L, DEC_BATCH, G, P), 0.1)
    inp['cache_conv'] = nrm((L, DEC_BATCH, CONV_WIDTH - 1, D_CONV), 0.5)
    inp['w_ada'] = nrm((L, D, N_MOD * D), 0.5 * D ** -0.5)
    inp['b_ada'] = nrm((L, N_MOD * D), 0.02)
    inp['g_ffn1'] = gain((L, D))
    inp['w1_ffn1'] = nrm((L, D, 2 * D_FF), D ** -0.5)
    inp['w2_ffn1'] = nrm((L, D_FF, D), D_FF ** -0.5)
    inp['g_mix'] = gain((L, D))
    inp['w_in'] = nrm((L, D, D_IN), D ** -0.5)
    inp['b_in'] = nrm((L, D_IN), 0.02)
    inp['lam_re'] = -0.5 + 0.02 * jax.random.normal(next(ks), (L, G, P), f32)
    inp['lam_im'] = jnp.pi * n_idx + 0.02 * jax.random.normal(next(ks), (L, G, P), f32)
    inp['log_dt'] = jax.random.uniform(next(ks), (L, G), f32, np.log(DT_MIN), np.log(DT_MAX))
    inp['b_re'] = nrm((L, G, P, M), (2.0 * M) ** -0.5)
    inp['b_im'] = nrm((L, G, P, M), (2.0 * M) ** -0.5)
    inp['c_re'] = nrm((L, G, M, P), (2.0 * P) ** -0.5)
    inp['c_im'] = nrm((L, G, M, P), (2.0 * P) ** -0.5)
    inp['d_skip'] = nrm((L, G, M), 1.0)
    inp['w_glu'] = nrm((L, D_SSM, 2 * D), D_SSM ** -0.5)
    inp['w_dw'] = nrm((L, CONV_WIDTH, D_CONV), CONV_WIDTH ** -0.5)
    inp['b_dw'] = nrm((L, D_CONV), 0.02)
    inp['ln_g'] = gain((L, D_CONV))
    inp['ln_b'] = nrm((L, D_CONV), 0.02)
    inp['w_pw'] = nrm((L, D_CONV, D), D_CONV ** -0.5)
    inp['b_pw'] = nrm((L, D), 0.02)
    inp['w_out'] = nrm((L, D, D), D ** -0.5)
    inp['g_ffn2'] = gain((L, D))
    inp['w1_ffn2'] = nrm((L, D, 2 * D_FF), D ** -0.5)
    inp['w2_ffn2'] = nrm((L, D_FF, D), D_FF ** -0.5)
    inp['g_final'] = gain((D,))
    inp['w_ada_f'] = nrm((D, 2 * D), 0.5 * D ** -0.5)
    inp['b_ada_f'] = nrm((2 * D,), 0.02)
    return inp


def reference(x_prompt, x_sample, c_prompt, c_sample, state_ssm_re, state_ssm_im, cache_conv,
              w_ada, b_ada, g_ffn1, w1_ffn1, w2_ffn1, g_mix, w_in, b_in,
              lam_re, lam_im, log_dt, b_re, b_im, c_re, c_im, d_skip, w_glu,
              w_dw, b_dw, ln_g, ln_b, w_pw, b_pw, w_out, g_ffn2, w1_ffn2, w2_ffn2,
              g_final, w_ada_f, b_ada_f):
    layer_params = (w_ada, b_ada, g_ffn1, w1_ffn1, w2_ffn1, g_mix, w_in, b_in,
                    lam_re, lam_im, log_dt, b_re, b_im, c_re, c_im, d_skip, w_glu,
                    w_dw, b_dw, ln_g, ln_b, w_pw, b_pw, w_out, g_ffn2, w1_ffn2, w2_ffn2)
    zs = jnp.zeros((DEPTH, x_prompt.shape[0], N_GROUPS, SSM_STATE), jnp.float32)
    zb = jnp.zeros((DEPTH, x_prompt.shape[0], CONV_WIDTH - 1, D_CONV), x_prompt.dtype)
    y_prompt, s_re_p, s_im_p, conv_p = _trunk(x_prompt, c_prompt, zs, zs, zb, layer_params,
                                              g_final, w_ada_f, b_ada_f)
    y_sample, s_re_s, s_im_s, conv_s = _trunk(x_sample, c_sample, state_ssm_re, state_ssm_im, cache_conv,
                                              layer_params, g_final, w_ada_f, b_ada_f)
    return (y_prompt, y_sample, s_re_p, s_im_p, conv_p, s_re_s, s_im_s, conv_s)
```

```python
import functools

import jax
import jax.numpy as jnp
from jax.experimental import pallas as pl
from jax.experimental.pallas import tpu as pltpu

D_MODEL = 1024
N_GROUPS = 64
SSM_GROUP = 16
SSM_STATE = 64
CONV_WIDTH = 31
D_FF = 2816
N_MOD = 9
EPS = 1e-6

SUBLANES = 8
LANES = 128
MXU_DIM = 256
VMEM_LIMIT_BYTES = 56 * 1024 * 1024

GROUPS_PER_BLOCK = MXU_DIM // SSM_GROUP
N_KBLOCKS = N_GROUPS // GROUPS_PER_BLOCK
STATE_PER_BLOCK = GROUPS_PER_BLOCK * SSM_STATE
N_STATE = N_GROUPS * SSM_STATE
HALO = 32
HALO_PAD = HALO - (CONV_WIDTH - 1)

BF16 = jnp.bfloat16
F32 = jnp.float32


def _const_spec(shape):
    return pl.BlockSpec(shape, lambda *_: (0,) * len(shape), pipeline_mode=pl.Buffered(1))


def _params(n_axes):
    return pltpu.CompilerParams(dimension_semantics=("arbitrary",) * n_axes,
                                vmem_limit_bytes=VMEM_LIMIT_BYTES)


def _rms_mod(x, g, shift, scale):
    y = x * jax.lax.rsqrt(jnp.mean(x * x, axis=-1, keepdims=True) + EPS) * g
    return y * (1.0 + scale) + shift


def _ada_body(c_ref, w_ref, b_ref, o_ref):
    a = jax.nn.silu(c_ref[...]).astype(BF16)
    o_ref[...] = jnp.dot(a, w_ref[...].astype(BF16), preferred_element_type=F32) + b_ref[...]


def _ada(c, w, b):
    r, n = c.shape[0], w.shape[1]
    tn = 1024
    return pl.pallas_call(
        _ada_body,
        grid=(n // tn,),
        in_specs=[pl.BlockSpec((r, D_MODEL), lambda j: (0, 0)),
                  pl.BlockSpec((D_MODEL, tn), lambda j: (0, j)),
                  pl.BlockSpec((1, tn), lambda j: (0, j))],
        out_specs=pl.BlockSpec((r, tn), lambda j: (0, j)),
        out_shape=jax.ShapeDtypeStruct((r, n), F32),
        compiler_params=_params(1),
        name="ada",
    )(c, w, b)


def _ffn_body(*refs, final):
    if final:
        x_ref, sh_ref, sc_ref, gt_ref, g_ref, w1_ref, w2_ref, gf_ref, shf_ref, scf_ref, o_ref = refs
    else:
        x_ref, sh_ref, sc_ref, gt_ref, g_ref, w1_ref, w2_ref, o_ref = refs
    x = x_ref[...]
    nb, t, d = x.shape
    n = _rms_mod(x, g_ref[...], sh_ref[...], sc_ref[...])
    gu = jnp.dot(n.reshape(nb * t, d).astype(BF16), w1_ref[...], preferred_element_type=F32)
    a = (jax.nn.silu(gu[:, :D_FF]) * gu[:, D_FF:]).astype(BF16)
    y = jnp.dot(a, w2_ref[...], preferred_element_type=F32).reshape(nb, t, d)
    h = x + 0.5 * gt_ref[...] * y
    if final:
        h = _rms_mod(h, gf_ref[...], shf_ref[...], scf_ref[...])
    o_ref[...] = h


def _ffn(x, shift, scale, gate, g, w1, w2, tile, final_args=None):
    nb_all, seq, d = x.shape
    nb, t = tile
    grid = (nb_all // nb, seq // t)
    row = pl.BlockSpec((nb, t, d), lambda b, i: (b, i, 0))
    per_seq = pl.BlockSpec((nb, 1, d), lambda b, i: (b, 0, 0))
    in_specs = [row, per_seq, per_seq, per_seq, _const_spec((1, d)),
                _const_spec(w1.shape), _const_spec(w2.shape)]
    args = [x, shift, scale, gate, g, w1, w2]
    if final_args is not None:
        in_specs += [_const_spec((1, d)), per_seq, per_seq]
        args += list(final_args)
    return pl.pallas_call(
        functools.partial(_ffn_body, final=final_args is not None),
        grid=grid, in_specs=in_specs, out_specs=row,
        out_shape=jax.ShapeDtypeStruct(x.shape, F32),
        compiler_params=_params(2),
        name="ffn_final" if final_args is not None else "ffn",
    )(*args)


def _inproj_body(h_ref, sh_ref, sc_ref, g_ref, w_ref, b_ref, u_ref, v_ref, gates_ref):
    h = h_ref[...]
    nb, t, d = h.shape
    n = _rms_mod(h, g_ref[...], sh_ref[...], sc_ref[...])
    proj = jnp.dot(n.reshape(nb * t, d).astype(BF16), w_ref[...], preferred_element_type=F32) + b_ref[...]
    u_ref[...] = proj[:, :d].reshape(nb, t, d)
    v = proj[:, d:2 * d] * jax.nn.sigmoid(proj[:, 2 * d:3 * d])
    v_ref[...] = v.reshape(nb, t, d)
    gates_ref[...] = jax.nn.sigmoid(proj[:, 3 * d:]).reshape(nb, t, 2 * d)


def _inproj(h, shift, scale, g, w_in, b_in, tile):
    nb_all, seq, d = h.shape
    nb, t = tile
    row = pl.BlockSpec((nb, t, d), lambda b, i: (b, i, 0))
    row2 = pl.BlockSpec((nb, t, 2 * d), lambda b, i: (b, i, 0))
    per_seq = pl.BlockSpec((nb, 1, d), lambda b, i: (b, 0, 0))
    return pl.pallas_call(
        _inproj_body,
        grid=(nb_all // nb, seq // t),
        in_specs=[row, per_seq, per_seq, _const_spec((1, d)), _const_spec(w_in.shape), _const_spec(b_in.shape)],
        out_specs=[row, row, row2],
        out_shape=[jax.ShapeDtypeStruct(h.shape, F32), jax.ShapeDtypeStruct(h.shape, F32),
                   jax.ShapeDtypeStruct((nb_all, seq, 2 * d), F32)],
        compiler_params=_params(2),
        name="inproj",
    )(h, shift, scale, g, w_in, b_in)


_C_A1R, _C_A1I, _C_A2R, _C_A2I, _C_A4R, _C_A4I, _C_PR, _C_PI = range(8)


def _scan_block(x_ref, consts_ref, cr_ref, ci_ref, kblock, rows):
    n_steps = rows // SUBLANES
    for j in range(STATE_PER_BLOCK // LANES):
        re_lanes = pl.ds(j * LANES, LANES)
        im_lanes = pl.ds(STATE_PER_BLOCK + j * LANES, LANES)
        c_lanes = pl.ds(kblock * STATE_PER_BLOCK + j * LANES, LANES)
        cst = [consts_ref[pl.ds(c * SUBLANES, SUBLANES), c_lanes] for c in range(8)]

        def step(s, carry, re_lanes=re_lanes, im_lanes=im_lanes, cst=cst):
            cr, ci = carry
            rows_s = pl.ds(pl.multiple_of(s * SUBLANES, SUBLANES), SUBLANES)
            r = x_ref[rows_s, re_lanes]
            i = x_ref[rows_s, im_lanes]
            for d, (ar, ai) in ((1, (cst[_C_A1R], cst[_C_A1I])), (2, (cst[_C_A2R], cst[_C_A2I])),
                                (4, (cst[_C_A4R], cst[_C_A4I]))):
                rs = pltpu.roll(r, d, 0)
                is_ = pltpu.roll(i, d, 0)
                r, i = r + ar * rs - ai * is_, i + ar * is_ + ai * rs
            r, i = (r + cst[_C_PR] * cr - cst[_C_PI] * ci,
                    i + cst[_C_PR] * ci + cst[_C_PI] * cr)
            x_ref[rows_s, re_lanes] = r
            x_ref[rows_s, im_lanes] = i
            return r[SUBLANES - 1:SUBLANES, :], i[SUBLANES - 1:SUBLANES, :]

        cr, ci = jax.lax.fori_loop(0, n_steps, step, (cr_ref[:, c_lanes], ci_ref[:, c_lanes]))
        cr_ref[:, c_lanes] = cr
        ci_ref[:, c_lanes] = ci


def _s5_body(u_ref, sr0_ref, si0_ref, bbd_ref, cbd_ref, consts_ref, dskip_ref,
             y_ref, sr_ref, si_ref, x_scr, cr_scr, ci_scr):
    i = pl.program_id(1)

    @pl.when(i == 0)
    def _():
        cr_scr[...] = sr0_ref[0]
        ci_scr[...] = si0_ref[0]

    u = u_ref[0]
    rows = u.shape[0]
    ub = u.astype(BF16)
    for k in range(N_KBLOCKS):
        cols = pl.ds(k * MXU_DIM, MXU_DIM)
        x_scr[...] = jnp.dot(ub[:, k * MXU_DIM:(k + 1) * MXU_DIM], bbd_ref[k], preferred_element_type=F32)
        _scan_block(x_scr, consts_ref, cr_scr, ci_scr, k, rows)
        yk = jnp.dot(x_scr[...].astype(BF16), cbd_ref[k], preferred_element_type=F32)
        y_ref[0, :, cols] = yk + dskip_ref[:, cols] * u[:, k * MXU_DIM:(k + 1) * MXU_DIM]
    sr_ref[0] = cr_scr[...]
    si_ref[0] = ci_scr[...]


def _s5(u, sr0, si0, bbd, cbd, consts, dskip, rows):
    nb_all, seq, d = u.shape
    row = pl.BlockSpec((1, rows, d), lambda b, i: (b, i, 0))
    st = pl.BlockSpec((1, 1, N_STATE), lambda b, i: (b, 0, 0))
    return pl.pallas_call(
        _s5_body,
        grid=(nb_all, seq // rows),
        in_specs=[row, st, st, _const_spec(bbd.shape), _const_spec(cbd.shape),
                  _const_spec(consts.shape), _const_spec(dskip.shape)],
        out_specs=[row, st, st],
        out_shape=[jax.ShapeDtypeStruct(u.shape, F32),
                   jax.ShapeDtypeStruct((nb_all, 1, N_STATE), F32),
                   jax.ShapeDtypeStruct((nb_all, 1, N_STATE), F32)],
        scratch_shapes=[pltpu.VMEM((rows, 2 * STATE_PER_BLOCK), F32),
                        pltpu.VMEM((1, N_STATE), F32), pltpu.VMEM((1, N_STATE), F32)],
        compiler_params=_params(2),
        name="s5",
    )(u, sr0, si0, bbd, cbd, consts, dskip)


def _s5_constants(lam_re, lam_im, log_dt, b_re, b_im, c_re, c_im, d_skip):
    dt = jnp.exp(log_dt)[:, None]
    mag = jnp.exp(lam_re * dt)
    ar = mag * jnp.cos(lam_im * dt)
    ai = mag * jnp.sin(lam_im * dt)
    den = lam_re * lam_re + lam_im * lam_im
    kr = ((ar - 1.0) * lam_re + ai * lam_im) / den
    ki = (ai * lam_re - (ar - 1.0) * lam_im) / den
    bbr = kr[..., None] * b_re - ki[..., None] * b_im
    bbi = kr[..., None] * b_im + ki[..., None] * b_re
    eye = jnp.eye(GROUPS_PER_BLOCK, dtype=F32)

    def b_blocks(bb):
        bb = bb.reshape(N_KBLOCKS, GROUPS_PER_BLOCK, SSM_STATE, SSM_GROUP)
        return jnp.einsum('kgpm,gh->kgmhp', bb, eye).reshape(N_KBLOCKS, MXU_DIM, STATE_PER_BLOCK)

    def c_blocks(cc):
        cc = cc.reshape(N_KBLOCKS, GROUPS_PER_BLOCK, SSM_GROUP, SSM_STATE)
        return jnp.einsum('kgmp,gh->kgphm', cc, eye).reshape(N_KBLOCKS, STATE_PER_BLOCK, MXU_DIM)

    bbd = jnp.concatenate([b_blocks(bbr), b_blocks(bbi)], axis=2).astype(BF16)
    cbd = jnp.concatenate([c_blocks(c_re), -c_blocks(c_im)], axis=1).astype(BF16)

    a1r, a1i = ar.reshape(1, N_STATE), ai.reshape(1, N_STATE)

    def cmul(x, y):
        return x[0] * y[0] - x[1] * y[1], x[0] * y[1] + x[1] * y[0]

    powers = [(a1r, a1i)]
    for _ in range(SUBLANES - 1):
        powers.append(cmul(powers[-1], (a1r, a1i)))
    row = jnp.arange(SUBLANES, dtype=jnp.int32)[:, None]

    def masked(p, d):
        return jnp.where(row >= d, p, 0.0)

    slabs = []
    for d in (1, 2, 4):
        slabs += [masked(powers[d - 1][0], d), masked(powers[d - 1][1], d)]
    slabs += [jnp.concatenate([p[0] for p in powers], axis=0), jnp.concatenate([p[1] for p in powers], axis=0)]
    consts = jnp.concatenate(slabs, axis=0)
    return bbd, cbd, consts, d_skip.reshape(1, D_MODEL)


def _mix_body(h_ref, ys_ref, v_ref, vprev_ref, halo0_ref, gates_ref, gt_ref,
              wglu_ref, wdw_ref, bdw_ref, lng_ref, lnb_ref, wpw_ref, bpw_ref, wout_ref,
              o_ref, ext_scr):
    i = pl.program_id(1)
    h = h_ref[...]
    nb, t, d = h.shape
    m = nb * t
    ga = jax.nn.gelu(ys_ref[...].reshape(m, d)).astype(BF16)
    ag = jnp.dot(ga, wglu_ref[...], preferred_element_type=F32)
    y_a = ag[:, :d] * jax.nn.sigmoid(ag[:, d:])
    ext_scr[:, :HALO, :] = jnp.where(i == 0, halo0_ref[...], vprev_ref[...])
    ext_scr[:, HALO:, :] = v_ref[...]
    acc = jnp.broadcast_to(bdw_ref[...].reshape(1, 1, d), (nb, t, d))
    for k in range(CONV_WIDTH):
        acc = acc + wdw_ref[k:k + 1, :].reshape(1, 1, d) * ext_scr[:, pl.ds(HALO_PAD + k, t), :]
    mu = jnp.mean(acc, axis=-1, keepdims=True)
    cen = acc - mu
    var = jnp.mean(cen * cen, axis=-1, keepdims=True)
    ln = cen * jax.lax.rsqrt(var + EPS) * lng_ref[...] + lnb_ref[...]
    vb = jax.nn.silu(ln).reshape(m, d).astype(BF16)
    y_b = jnp.dot(vb, wpw_ref[...], preferred_element_type=F32) + bpw_ref[...]
    gates = gates_ref[...].reshape(m, 2 * d)
    merged = (gates[:, :d] * y_a + gates[:, d:] * y_b).astype(BF16)
    out = jnp.dot(merged, wout_ref[...], preferred_element_type=F32).reshape(nb, t, d)
    o_ref[...] = h + gt_ref[...] * out


def _mix(h, ys, v, halo0, gates, gate2, w_glu, w_dw, b_dw, ln_g, ln_b, w_pw, b_pw, w_out, tile):
    nb_all, seq, d = h.shape
    nb, t = tile
    halo_blocks_per_tile = t // HALO
    row = pl.BlockSpec((nb, t, d), lambda b, i: (b, i, 0))
    row2 = pl.BlockSpec((nb, t, 2 * d), lambda b, i: (b, i, 0))
    per_seq = pl.BlockSpec((nb, 1, d), lambda b, i: (b, 0, 0))
    prev = pl.BlockSpec((nb, HALO, d), lambda b, i: (b, jnp.maximum(i * halo_blocks_per_tile - 1, 0), 0))
    halo_spec = pl.BlockSpec((nb, HALO, d), lambda b, i: (b, 0, 0))
    return pl.pallas_call(
        _mix_body,
        grid=(nb_all // nb, seq // t),
        in_specs=[row, row, row, prev, halo_spec, row2, per_seq,
                  _const_spec(w_glu.shape), _const_spec(w_dw.shape), _const_spec((1, d)),
                  _const_spec((1, d)), _const_spec((1, d)), _const_spec(w_pw.shape),
                  _const_spec((1, d)), _const_spec(w_out.shape)],
        out_specs=row,
        out_shape=jax.ShapeDtypeStruct(h.shape, F32),
        scratch_shapes=[pltpu.VMEM((nb, t + HALO, d), F32)],
        compiler_params=_params(2),
        name="mix",
    )(h, ys, v, v, halo0, gates, gate2, w_glu, w_dw, b_dw, ln_g, ln_b, w_pw, b_pw, w_out)


def _trunk(x, mod, modf, s_re, s_im, conv_buf, p, tile, s5_rows):
    m = [mod[:, j:j + 1, :] for j in range(N_MOD)]
    h1 = _ffn(x, m[0], m[1], m[2], p['g_ffn1'], p['w1_ffn1'], p['w2_ffn1'], tile)
    u, v, gates = _inproj(h1, m[3], m[4], p['g_mix'], p['w_in'], p['b_in'], tile)
    ys, new_re, new_im = _s5(u, s_re, s_im, p['bbd'], p['cbd'], p['consts'], p['dskip'], s5_rows)
    halo0 = jnp.pad(conv_buf, ((0, 0), (HALO_PAD, 0), (0, 0)))
    h2 = _mix(h1, ys, v, halo0, gates, m[5], p['w_glu'], p['w_dw'], p['b_dw'], p['ln_g'], p['ln_b'],
              p['w_pw'], p['b_pw'], p['w_out'], tile)
    y = _ffn(h2, m[6], m[7], m[8], p['g_ffn2'], p['w1_ffn2'], p['w2_ffn2'], tile,
             final_args=(p['g_final'], modf[:, 0:1, :], modf[:, 1:2, :]))
    nb = x.shape[0]
    new_buf = jnp.concatenate([conv_buf, v], axis=1)[:, -(CONV_WIDTH - 1):]
    return (y, new_re.reshape(1, nb, N_GROUPS, SSM_STATE), new_im.reshape(1, nb, N_GROUPS, SSM_STATE),
            new_buf[None])


def kernel(x_prompt, x_sample, c_prompt, c_sample, state_ssm_re, state_ssm_im, cache_conv, w_ada, b_ada, g_ffn1, w1_ffn1, w2_ffn1, g_mix, w_in, b_in, lam_re, lam_im, log_dt, b_re, b_im, c_re, c_im, d_skip, w_glu, w_dw, b_dw, ln_g, ln_b, w_pw, b_pw, w_out, g_ffn2, w1_ffn2, w2_ffn2, g_final, w_ada_f, b_ada_f):
    assert w_ada.shape[0] == 1, "single layer"
    d = D_MODEL
    bp, bs = x_prompt.shape[0], x_sample.shape[0]
    row = lambda a: a.reshape(1, -1)
    bbd, cbd, consts, dskip = _s5_constants(lam_re[0], lam_im[0], log_dt[0], b_re[0], b_im[0],
                                            c_re[0], c_im[0], d_skip[0])
    p = dict(
        g_ffn1=row(g_ffn1[0]), w1_ffn1=w1_ffn1[0].astype(BF16), w2_ffn1=w2_ffn1[0].astype(BF16),
        g_mix=row(g_mix[0]), w_in=w_in[0].astype(BF16), b_in=row(b_in[0]),
        bbd=bbd, cbd=cbd, consts=consts, dskip=dskip,
        w_glu=w_glu[0].astype(BF16), w_dw=w_dw[0], b_dw=row(b_dw[0]), ln_g=row(ln_g[0]), ln_b=row(ln_b[0]),
        w_pw=w_pw[0].astype(BF16), b_pw=row(b_pw[0]), w_out=w_out[0].astype(BF16),
        g_ffn2=row(g_ffn2[0]), w1_ffn2=w1_ffn2[0].astype(BF16), w2_ffn2=w2_ffn2[0].astype(BF16),
        g_final=row(g_final),
    )
    n_c = bp + bs
    c_all = jnp.pad(jnp.concatenate([c_prompt, c_sample], axis=0), ((0, (-n_c) % SUBLANES), (0, 0)))
    mod = _ada(c_all, w_ada[0], row(b_ada[0]))[:n_c].reshape(n_c, N_MOD, d)
    modf = _ada(c_all, w_ada_f, row(b_ada_f))[:n_c].reshape(n_c, 2, d)

    zs = jnp.zeros((bp, 1, N_STATE), F32)
    zb = jnp.zeros((bp, CONV_WIDTH - 1, d), F32)
    out_p = _trunk(x_prompt, mod[:bp], modf[:bp], zs, zs, zb, p, tile=(1, 512), s5_rows=256)
    out_s = _trunk(x_sample, mod[bp:], modf[bp:],
                   state_ssm_re[0].reshape(bs, 1, N_STATE), state_ssm_im[0].reshape(bs, 1, N_STATE),
                   cache_conv[0], p, tile=(bs, x_sample.shape[1]), s5_rows=x_sample.shape[1])
    return (out_p[0], out_s[0], out_p[1], out_p[2], out_p[3], out_s[1], out_s[2], out_s[3])
```

```python
import functools

import jax
import jax.numpy as jnp
from jax.experimental import pallas as pl
from jax.experimental.pallas import tpu as pltpu

D_MODEL = 1024
N_GROUPS = 64
SSM_GROUP = 16
SSM_STATE = 64
CONV_WIDTH = 31
D_FF = 2816
N_MOD = 9
EPS = 1e-6

SUBLANES = 8
LANES = 128
MXU_DIM = 256
VMEM_LIMIT_BYTES = 56 * 1024 * 1024

GROUPS_PER_BLOCK = MXU_DIM // SSM_GROUP
N_KBLOCKS = N_GROUPS // GROUPS_PER_BLOCK
STATE_PER_BLOCK = GROUPS_PER_BLOCK * SSM_STATE
N_STATE = N_GROUPS * SSM_STATE
HIST = CONV_WIDTH - 1
TILE_STEPS = 64
SCAN_LANE_BLOCKS = 4
CONV_CHUNK = 4

BF16 = jnp.bfloat16
F32 = jnp.float32


def _const_spec(shape):
    return pl.BlockSpec(shape, lambda *_: (0,) * len(shape), pipeline_mode=pl.Buffered(1))


def _params():
    return pltpu.CompilerParams(dimension_semantics=("arbitrary",), vmem_limit_bytes=VMEM_LIMIT_BYTES)


def _rms_mod(x, g, shift, scale):
    y = x * jax.lax.rsqrt(jnp.mean(x * x, axis=-1, keepdims=True) + EPS) * g
    return y * (1.0 + scale) + shift


def _tok_spec(ts, width):
    return pl.BlockSpec((ts, SUBLANES, width), lambda i: (i, 0, 0))


def _ada_body(c_ref, w_ref, b_ref, o_ref):
    a = jax.nn.silu(c_ref[...]).astype(BF16)
    o_ref[...] = jnp.dot(a, w_ref[...].astype(BF16), preferred_element_type=F32) + b_ref[...]


def _ada(c, w, b):
    r, n = c.shape[0], w.shape[1]
    tn = 1024
    return pl.pallas_call(
        _ada_body,
        grid=(n // tn,),
        in_specs=[pl.BlockSpec((r, D_MODEL), lambda j: (0, 0)),
                  pl.BlockSpec((D_MODEL, tn), lambda j: (0, j)),
                  pl.BlockSpec((1, tn), lambda j: (0, j))],
        out_specs=pl.BlockSpec((r, tn), lambda j: (0, j)),
        out_shape=jax.ShapeDtypeStruct((r, n), F32),
        compiler_params=_params(),
        name="ada",
    )(c, w, b)


def _ffn_body(*refs, final):
    if final:
        x_ref, sh_ref, sc_ref, gt_ref, g_ref, w1_ref, w2_ref, gf_ref, shf_ref, scf_ref, o_ref = refs
    else:
        x_ref, sh_ref, sc_ref, gt_ref, g_ref, w1_ref, w2_ref, o_ref = refs
    x = x_ref[...]
    ts, s, d = x.shape
    n = _rms_mod(x, g_ref[...], sh_ref[...], sc_ref[...])
    gu = jnp.dot(n.reshape(ts * s, d).astype(BF16), w1_ref[...], preferred_element_type=F32)
    a = (jax.nn.silu(gu[:, :D_FF]) * gu[:, D_FF:]).astype(BF16)
    y = jnp.dot(a, w2_ref[...], preferred_element_type=F32).reshape(ts, s, d)
    h = x + 0.5 * gt_ref[...] * y
    if final:
        h = _rms_mod(h, gf_ref[...], shf_ref[...], scf_ref[...])
    o_ref[...] = h


def _ffn(x, shift, scale, gate, g, w1, w2, final_args=None):
    r, s, d = x.shape
    ts = min(TILE_STEPS, r)
    tok = _tok_spec(ts, d)
    per_stream = _const_spec((s, d))
    in_specs = [tok, per_stream, per_stream, per_stream, _const_spec((1, d)),
                _const_spec(w1.shape), _const_spec(w2.shape)]
    args = [x, shift, scale, gate, g, w1, w2]
    if final_args is not None:
        in_specs += [_const_spec((1, d)), per_stream, per_stream]
        args += list(final_args)
    return pl.pallas_call(
        functools.partial(_ffn_body, final=final_args is not None),
        grid=(r // ts,), in_specs=in_specs, out_specs=tok,
        out_shape=jax.ShapeDtypeStruct(x.shape, F32),
        compiler_params=_params(),
        name="ffn_final" if final_args is not None else "ffn",
    )(*args)


def _inproj_body(h_ref, sh_ref, sc_ref, g_ref, w_ref, b_ref, u_ref, v_ref, gates_ref):
    h = h_ref[...]
    ts, s, d = h.shape
    n = _rms_mod(h, g_ref[...], sh_ref[...], sc_ref[...])
    proj = jnp.dot(n.reshape(ts * s, d).astype(BF16), w_ref[...], preferred_element_type=F32) + b_ref[...]
    u_ref[...] = proj[:, :d].reshape(ts, s, d)
    v = proj[:, d:2 * d] * jax.nn.sigmoid(proj[:, 2 * d:3 * d])
    v_ref[...] = v.reshape(ts, s, d)
    gates_ref[...] = jax.nn.sigmoid(proj[:, 3 * d:]).reshape(ts, s, 2 * d)


def _inproj(h, shift, scale, g, w_in, b_in):
    r, s, d = h.shape
    ts = min(TILE_STEPS, r)
    tok = _tok_spec(ts, d)
    per_stream = _const_spec((s, d))
    return pl.pallas_call(
        _inproj_body,
        grid=(r // ts,),
        in_specs=[tok, per_stream, per_stream, _const_spec((1, d)), _const_spec(w_in.shape), _const_spec(b_in.shape)],
        out_specs=[tok, tok, _tok_spec(ts, 2 * d)],
        out_shape=[jax.ShapeDtypeStruct(h.shape, F32), jax.ShapeDtypeStruct(h.shape, F32),
                   jax.ShapeDtypeStruct((r, s, 2 * d), F32)],
        compiler_params=_params(),
        name="inproj",
    )(h, shift, scale, g, w_in, b_in)


_F_A1R, _F_A1I, _F_A2R, _F_A2I, _F_A4R, _F_A4I, _F_AR, _F_AI = range(8)


def _cmul(ar, ai, xr, xi):
    return ar * xr - ai * xi, ar * xi + ai * xr


def _scan_group(x_scr, ab_ref, fix_ref, cr_scr, ci_scr, kblock, group, steps, chained):
    blocks = range(group * SCAN_LANE_BLOCKS, (group + 1) * SCAN_LANE_BLOCKS)
    re_l = [pl.ds(j * LANES, LANES) for j in blocks]
    im_l = [pl.ds(STATE_PER_BLOCK + j * LANES, LANES) for j in blocks]
    c_l = [pl.ds(kblock * STATE_PER_BLOCK + j * LANES, LANES) for j in blocks]
    ar = [ab_ref[0, :, c] for c in c_l]
    ai = [ab_ref[1, :, c] for c in c_l]
    n = len(re_l)

    def rows(t):
        return pl.ds(pl.multiple_of(t * SUBLANES, SUBLANES), SUBLANES)

    def sweep(t, s):
        out = []
        for j in range(n):
            pr, pi = _cmul(ar[j], ai[j], s[2 * j], s[2 * j + 1])
            nr = pr + x_scr[rows(t), re_l[j]]
            ni = pi + x_scr[rows(t), im_l[j]]
            x_scr[rows(t), re_l[j]] = nr
            x_scr[rows(t), im_l[j]] = ni
            out += [nr, ni]
        return tuple(out)

    if not chained:
        s0 = []
        for j in range(n):
            s0 += [cr_scr[:, c_l[j]], ci_scr[:, c_l[j]]]
        s = jax.lax.fori_loop(0, steps, sweep, tuple(s0), unroll=True)
        for j in range(n):
            cr_scr[:, c_l[j]] = s[2 * j]
            ci_scr[:, c_l[j]] = s[2 * j + 1]
        return

    zero = jnp.zeros((SUBLANES, LANES), F32)
    ends = jax.lax.fori_loop(0, steps, sweep, (zero,) * (2 * n), unroll=True)
    sub = jax.lax.broadcasted_iota(jnp.int32, (SUBLANES, LANES), 0)
    starts = []
    for j in range(n):
        fx = [fix_ref[pl.ds(c * SUBLANES, SUBLANES), c_l[j]] for c in range(8)]
        er, ei = ends[2 * j], ends[2 * j + 1]
        kr = jnp.where(sub == 0, cr_scr[0:1, c_l[j]], pltpu.roll(er, 1, 0))
        ki = jnp.where(sub == 0, ci_scr[0:1, c_l[j]], pltpu.roll(ei, 1, 0))
        for d, (fr, fi) in ((1, (fx[_F_A1R], fx[_F_A1I])), (2, (fx[_F_A2R], fx[_F_A2I])),
                            (4, (fx[_F_A4R], fx[_F_A4I]))):
            pr, pi = _cmul(fr, fi, pltpu.roll(kr, d, 0), pltpu.roll(ki, d, 0))
            kr, ki = kr + pr, ki + pi
        pr, pi = _cmul(fx[_F_AR], fx[_F_AI], kr, ki)
        cr_scr[0:1, c_l[j]] = (pr + er)[SUBLANES - 1:SUBLANES, :]
        ci_scr[0:1, c_l[j]] = (pi + ei)[SUBLANES - 1:SUBLANES, :]
        starts += list(_cmul(ar[j], ai[j], kr, ki))

    def fixup(t, v):
        out = []
        for j in range(n):
            x_scr[rows(t), re_l[j]] = x_scr[rows(t), re_l[j]] + v[2 * j]
            x_scr[rows(t), im_l[j]] = x_scr[rows(t), im_l[j]] + v[2 * j + 1]
            out += list(_cmul(ar[j], ai[j], v[2 * j], v[2 * j + 1]))
        return tuple(out)

    jax.lax.fori_loop(0, steps, fixup, tuple(starts), unroll=True)


def _s5_body(u_ref, sr0_ref, si0_ref, bbd_ref, cbd_ref, ab_ref, fix_ref, dskip_ref,
             y_ref, sr_ref, si_ref, x_scr, cr_scr, ci_scr, *, chained):
    @pl.when(pl.program_id(0) == 0)
    def _():
        cr_scr[...] = sr0_ref[...]
        ci_scr[...] = si0_ref[...]

    ts, s, d = u_ref.shape
    u = u_ref[...].reshape(ts * s, d)
    ub = u.astype(BF16)
    for k in range(N_KBLOCKS):
        cols = slice(k * MXU_DIM, (k + 1) * MXU_DIM)
        x_scr[...] = jnp.dot(ub[:, cols], bbd_ref[k], preferred_element_type=F32)
        for group in range(STATE_PER_BLOCK // LANES // SCAN_LANE_BLOCKS):
            _scan_group(x_scr, ab_ref, fix_ref, cr_scr, ci_scr, k, group, ts, chained)
        yk = jnp.dot(x_scr[...].astype(BF16), cbd_ref[k], preferred_element_type=F32)
        y_ref[:, :, cols] = (yk + dskip_ref[:, cols] * u[:, cols]).reshape(ts, s, MXU_DIM)
    sr_ref[...] = cr_scr[...]
    si_ref[...] = ci_scr[...]


def _s5(u, sr0, si0, sc, chained):
    r, s, d = u.shape
    ts = min(TILE_STEPS, r)
    tok = _tok_spec(ts, d)
    st = _const_spec((s, N_STATE))
    return pl.pallas_call(
        functools.partial(_s5_body, chained=chained),
        grid=(r // ts,),
        in_specs=[tok, st, st, _const_spec(sc['bbd'].shape), _const_spec(sc['cbd'].shape),
                  _const_spec(sc['ab'].shape), _const_spec(sc['fix'].shape), _const_spec(sc['dskip'].shape)],
        out_specs=[tok, pl.BlockSpec((s, N_STATE), lambda i: (0, 0)), pl.BlockSpec((s, N_STATE), lambda i: (0, 0))],
        out_shape=[jax.ShapeDtypeStruct(u.shape, F32),
                   jax.ShapeDtypeStruct((s, N_STATE), F32), jax.ShapeDtypeStruct((s, N_STATE), F32)],
        scratch_shapes=[pltpu.VMEM((ts * s, 2 * STATE_PER_BLOCK), F32),
                        pltpu.VMEM((s, N_STATE), F32), pltpu.VMEM((s, N_STATE), F32)],
        compiler_params=_params(),
        name="s5_chained" if chained else "s5",
    )(u, sr0, si0, sc['bbd'], sc['cbd'], sc['ab'], sc['fix'], sc['dskip'])


def _s5_constants(lam_re, lam_im, log_dt, b_re, b_im, c_re, c_im, d_skip, steps):
    dt = jnp.exp(log_dt)[:, None]
    mag = jnp.exp(lam_re * dt)
    ar = mag * jnp.cos(lam_im * dt)
    ai = mag * jnp.sin(lam_im * dt)
    den = lam_re * lam_re + lam_im * lam_im
    kr = ((ar - 1.0) * lam_re + ai * lam_im) / den
    ki = (ai * lam_re - (ar - 1.0) * lam_im) / den
    bbr = kr[..., None] * b_re - ki[..., None] * b_im
    bbi = kr[..., None] * b_im + ki[..., None] * b_re
    eye = jnp.eye(GROUPS_PER_BLOCK, dtype=F32)

    def b_blocks(bb):
        bb = bb.reshape(N_KBLOCKS, GROUPS_PER_BLOCK, SSM_STATE, SSM_GROUP)
        return jnp.einsum('kgpm,gh->kgmhp', bb, eye).reshape(N_KBLOCKS, MXU_DIM, STATE_PER_BLOCK)

    def c_blocks(cc):
        cc = cc.reshape(N_KBLOCKS, GROUPS_PER_BLOCK, SSM_GROUP, SSM_STATE)
        return jnp.einsum('kgmp,gh->kgphm', cc, eye).reshape(N_KBLOCKS, STATE_PER_BLOCK, MXU_DIM)

    bbd = jnp.concatenate([b_blocks(bbr), b_blocks(bbi)], axis=2).astype(BF16)
    cbd = jnp.concatenate([c_blocks(c_re), -c_blocks(c_im)], axis=1).astype(BF16)

    a = (ar.reshape(1, N_STATE), ai.reshape(1, N_STATE))
    ab = jnp.stack([jnp.broadcast_to(a[0], (SUBLANES, N_STATE)), jnp.broadcast_to(a[1], (SUBLANES, N_STATE))])

    def csq(x):
        return _cmul(x[0], x[1], x[0], x[1])

    assert steps & (steps - 1) == 0, "a^steps by repeated squaring"
    big = a
    for _ in range(steps.bit_length() - 1):
        big = csq(big)
    big2 = csq(big)
    big4 = csq(big2)
    sub = jnp.arange(SUBLANES, dtype=jnp.int32)[:, None]
    slabs = []
    for d, pw in ((1, big), (2, big2), (4, big4)):
        slabs += [jnp.where(sub >= d, pw[0], 0.0), jnp.where(sub >= d, pw[1], 0.0)]
    slabs += [jnp.broadcast_to(big[0], (SUBLANES, N_STATE)), jnp.broadcast_to(big[1], (SUBLANES, N_STATE))]
    fix = jnp.concatenate(slabs, axis=0)
    return dict(bbd=bbd, cbd=cbd, ab=ab, fix=fix, dskip=d_skip.reshape(1, D_MODEL))


def _mix_body(h_ref, ys_ref, v_ref, hist0_ref, gates_ref, gt_ref,
              wglu_ref, wdw_ref, bdw_ref, lng_ref, lnb_ref, wpw_ref, bpw_ref, wout_ref,
              o_ref, ext_scr, tail_scr, conv_scr, *, chained):
    ts, s, d = h_ref.shape
    m = ts * s

    @pl.when(pl.program_id(0) == 0)
    def _():
        tail_scr[...] = hist0_ref[...]

    cur_tail = v_ref[pl.ds(ts - HIST, HIST), :, :]
    if chained:
        sub = jax.lax.broadcasted_iota(jnp.int32, (HIST, s, d), 1)
        hist = jnp.where(sub == 0, pltpu.roll(tail_scr[...], 1, 1), pltpu.roll(cur_tail, 1, 1))
    else:
        hist = tail_scr[...]
    ext_scr[pl.ds(0, HIST), :, :] = hist
    ext_scr[pl.ds(HIST, ts), :, :] = v_ref[...]
    tail_scr[...] = cur_tail

    def conv_chunk(ci, carry):
        t0 = ci * CONV_CHUNK
        acc = jnp.broadcast_to(bdw_ref[...].reshape(1, 1, d), (CONV_CHUNK, s, d))
        for k in range(CONV_WIDTH):
            acc = acc + wdw_ref[k][None] * ext_scr[pl.ds(t0 + k, CONV_CHUNK), :, :]
        conv_scr[pl.ds(t0, CONV_CHUNK), :, :] = acc
        return carry

    jax.lax.fori_loop(0, ts // CONV_CHUNK, conv_chunk, 0, unroll=True)

    conv = conv_scr[...]
    mu = jnp.mean(conv, axis=-1, keepdims=True)
    cen = conv - mu
    var = jnp.mean(cen * cen, axis=-1, keepdims=True)
    ln = cen * jax.lax.rsqrt(var + EPS) * lng_ref[...] + lnb_ref[...]
    vb = jax.nn.silu(ln).reshape(m, d).astype(BF16)
    y_b = jnp.dot(vb, wpw_ref[...], preferred_element_type=F32) + bpw_ref[...]
    ga = jax.nn.gelu(ys_ref[...].reshape(m, d)).astype(BF16)
    ag = jnp.dot(ga, wglu_ref[...], preferred_element_type=F32)
    y_a = ag[:, :d] * jax.nn.sigmoid(ag[:, d:])
    gates = gates_ref[...].reshape(m, 2 * d)
    merged = (gates[:, :d] * y_a + gates[:, d:] * y_b).astype(BF16)
    out = jnp.dot(merged, wout_ref[...], preferred_element_type=F32).reshape(ts, s, d)
    o_ref[...] = h_ref[...] + gt_ref[...] * out


def _mix(h, ys, v, hist0, gates, gate2, p, chained):
    r, s, d = h.shape
    ts = min(TILE_STEPS, r)
    tok = _tok_spec(ts, d)
    return pl.pallas_call(
        functools.partial(_mix_body, chained=chained),
        grid=(r // ts,),
        in_specs=[tok, tok, tok, _const_spec(hist0.shape), _tok_spec(ts, 2 * d), _const_spec((s, d)),
                  _const_spec(p['w_glu'].shape), _const_spec(p['w_dw'].shape), _const_spec((1, d)),
                  _const_spec((1, d)), _const_spec((1, d)), _const_spec(p['w_pw'].shape),
                  _const_spec((1, d)), _const_spec(p['w_out'].shape)],
        out_specs=tok,
        out_shape=jax.ShapeDtypeStruct(h.shape, F32),
        scratch_shapes=[pltpu.VMEM((ts + HIST, s, d), F32), pltpu.VMEM((HIST, s, d), F32),
                        pltpu.VMEM((ts, s, d), F32)],
        compiler_params=_params(),
        name="mix_chained" if chained else "mix",
    )(h, ys, v, hist0, gates, gate2, p['w_glu'], p['w_dw'], p['b_dw'], p['ln_g'], p['ln_b'],
      p['w_pw'], p['b_pw'], p['w_out'])


def _trunk(x, mod, modf, s_re, s_im, hist0, p, sc, chained):
    m = [mod[:, j, :] for j in range(N_MOD)]
    h1 = _ffn(x, m[0], m[1], m[2], p['g_ffn1'], p['w1_ffn1'], p['w2_ffn1'])
    u, v, gates = _inproj(h1, m[3], m[4], p['g_mix'], p['w_in'], p['b_in'])
    ys, new_re, new_im = _s5(u, s_re, s_im, sc, chained)
    h2 = _mix(h1, ys, v, hist0, gates, m[5], p, chained)
    y = _ffn(h2, m[6], m[7], m[8], p['g_ffn2'], p['w1_ffn2'], p['w2_ffn2'],
             final_args=(p['g_final'], modf[:, 0, :], modf[:, 1, :]))
    return y, new_re, new_im, v


def kernel(x_prompt, x_sample, c_prompt, c_sample, state_ssm_re, state_ssm_im, cache_conv, w_ada, b_ada, g_ffn1, w1_ffn1, w2_ffn1, g_mix, w_in, b_in, lam_re, lam_im, log_dt, b_re, b_im, c_re, c_im, d_skip, w_glu, w_dw, b_dw, ln_g, ln_b, w_pw, b_pw, w_out, g_ffn2, w1_ffn2, w2_ffn2, g_final, w_ada_f, b_ada_f):
    d, s = D_MODEL, SUBLANES
    bp, lp = x_prompt.shape[:2]
    bs, ls = x_sample.shape[:2]
    assert w_ada.shape[0] == 1 and bp == 1 and bs == s, "single layer, one prompt, 8 sample sequences"
    tile = s * TILE_STEPS
    assert lp % tile == 0 and ls <= TILE_STEPS and ls >= HIST and TILE_STEPS >= HIST
    row = lambda a: a.reshape(1, -1)
    p = dict(
        g_ffn1=row(g_ffn1[0]), w1_ffn1=w1_ffn1[0].astype(BF16), w2_ffn1=w2_ffn1[0].astype(BF16),
        g_mix=row(g_mix[0]), w_in=w_in[0].astype(BF16), b_in=row(b_in[0]),
        w_glu=w_glu[0].astype(BF16), w_dw=jnp.broadcast_to(w_dw[0][:, None, :], (CONV_WIDTH, s, d)),
        b_dw=row(b_dw[0]), ln_g=row(ln_g[0]), ln_b=row(ln_b[0]),
        w_pw=w_pw[0].astype(BF16), b_pw=row(b_pw[0]), w_out=w_out[0].astype(BF16),
        g_ffn2=row(g_ffn2[0]), w1_ffn2=w1_ffn2[0].astype(BF16), w2_ffn2=w2_ffn2[0].astype(BF16),
        g_final=row(g_final),
    )
    sc = _s5_constants(lam_re[0], lam_im[0], log_dt[0], b_re[0], b_im[0], c_re[0], c_im[0], d_skip[0], TILE_STEPS)

    n_c = bp + bs
    c_all = jnp.pad(jnp.concatenate([c_prompt, c_sample], axis=0), ((0, (-n_c) % s), (0, 0)))
    mod = _ada(c_all, w_ada[0], row(b_ada[0]))[:n_c].reshape(n_c, N_MOD, d)
    modf = _ada(c_all, w_ada_f, row(b_ada_f))[:n_c].reshape(n_c, 2, d)

    n_tiles = lp // tile
    xp = x_prompt.reshape(n_tiles, s, TILE_STEPS, d).transpose(0, 2, 1, 3).reshape(n_tiles * TILE_STEPS, s, d)
    zs = jnp.zeros((s, N_STATE), F32)
    hist_p = jnp.zeros((HIST, s, d), F32)
    yp, re_p, im_p, vp = _trunk(xp, jnp.broadcast_to(mod[:bp], (s, N_MOD, d)), jnp.broadcast_to(modf[:bp], (s, 2, d)),
                                zs, zs, hist_p, p, sc, chained=True)
    y_prompt = yp.reshape(n_tiles, TILE_STEPS, s, d).transpose(0, 2, 1, 3).reshape(bp, lp, d)
    conv_p = vp[-HIST:, s - 1, :]

    xs = x_sample.transpose(1, 0, 2)
    ys_, re_s, im_s, vs = _trunk(xs, mod[bp:], modf[bp:],
                                 state_ssm_re[0].reshape(bs, N_STATE), state_ssm_im[0].reshape(bs, N_STATE),
                                 cache_conv[0].transpose(1, 0, 2), p, sc, chained=False)
    y_sample = ys_.transpose(1, 0, 2)
    conv_s = vs[-HIST:].transpose(1, 0, 2)

    st = lambda a, n: a[:n].reshape(1, n, N_GROUPS, SSM_STATE)
    return (y_prompt, y_sample, st(re_p, bp), st(im_p, bp), conv_p.reshape(1, bp, HIST, d),
            st(re_s, bs), st(im_s, bs), conv_s[None])
```

```python
import functools

import jax
import jax.numpy as jnp
from jax.experimental import pallas as pl
from jax.experimental.pallas import tpu as pltpu

D_MODEL = 1024
N_GROUPS = 64
SSM_GROUP = 16
SSM_STATE = 64
CONV_WIDTH = 31
D_FF = 2816
N_MOD = 9
EPS = 1e-6

SUBLANES = 8
LANES = 128
MXU_DIM = 256
VMEM_LIMIT_BYTES = 56 * 1024 * 1024

GROUPS_PER_BLOCK = MXU_DIM // SSM_GROUP
N_KBLOCKS = N_GROUPS // GROUPS_PER_BLOCK
STATE_PER_BLOCK = GROUPS_PER_BLOCK * SSM_STATE
N_STATE = N_GROUPS * SSM_STATE
HIST = CONV_WIDTH - 1
TILE_STEPS = 64
SCAN_LANE_BLOCKS = 4
CONV_CHUNK = 16
CONV_TAPS = 16
SAMPLE_GROUP, PROMPT_GROUP = 0, 1

BF16 = jnp.bfloat16
F32 = jnp.float32


def _const_spec(shape):
    return pl.BlockSpec(shape, lambda *_: (0,) * len(shape), pipeline_mode=pl.Buffered(1))


def _params():
    return pltpu.CompilerParams(dimension_semantics=("arbitrary",), vmem_limit_bytes=VMEM_LIMIT_BYTES)


def _rms_mod(x, g, shift, scale):
    y = x * jax.lax.rsqrt(jnp.mean(x * x, axis=-1, keepdims=True) + EPS) * g
    return y * (1.0 + scale) + shift


def _tok_spec(width, tile0=0, last=None):
    if last is None:
        return pl.BlockSpec((TILE_STEPS, SUBLANES, width), lambda i: (i + tile0, 0, 0))
    return pl.BlockSpec((TILE_STEPS, SUBLANES, width), lambda i: (jnp.minimum(i + tile0, last), 0, 0))


def _mod_spec(j, n_prompt_tiles=None, group=None):
    if group is not None:
        return pl.BlockSpec((SUBLANES, D_MODEL), lambda i: (group, j))
    return pl.BlockSpec((SUBLANES, D_MODEL),
                        lambda i: (jnp.where(i < n_prompt_tiles, PROMPT_GROUP, SAMPLE_GROUP), j))


def _ada_body(c_ref, w_ref, b_ref, o_ref):
    a = jax.nn.silu(c_ref[...]).astype(BF16)
    o_ref[...] = jnp.dot(a, w_ref[...].astype(BF16), preferred_element_type=F32) + b_ref[...]


def _ada(c, w, b):
    r, n = c.shape[0], w.shape[1]
    tn = 1024
    return pl.pallas_call(
        _ada_body,
        grid=(n // tn,),
        in_specs=[pl.BlockSpec((r, D_MODEL), lambda j: (0, 0)),
                  pl.BlockSpec((D_MODEL, tn), lambda j: (0, j)),
                  pl.BlockSpec((1, tn), lambda j: (0, j))],
        out_specs=pl.BlockSpec((r, tn), lambda j: (0, j)),
        out_shape=jax.ShapeDtypeStruct((r, n), F32),
        compiler_params=_params(),
        name="ada",
    )(c, w, b)


def _ffn_body(*refs, final, n_prompt_tiles):
    if final:
        xp_ref, xs_ref, sh_ref, sc_ref, gt_ref, g_ref, w1_ref, w2_ref, gf_ref, shf_ref, scf_ref, o_ref = refs
    else:
        xp_ref, xs_ref, sh_ref, sc_ref, gt_ref, g_ref, w1_ref, w2_ref, o_ref = refs
    x = jnp.where(pl.program_id(0) < n_prompt_tiles, xp_ref[...], xs_ref[...])
    ts, s, d = x.shape
    n = _rms_mod(x, g_ref[...], sh_ref[...], sc_ref[...])
    gu = jnp.dot(n.reshape(ts * s, d).astype(BF16), w1_ref[...], preferred_element_type=F32)
    a = (jax.nn.silu(gu[:, :D_FF]) * gu[:, D_FF:]).astype(BF16)
    y = jnp.dot(a, w2_ref[...], preferred_element_type=F32).reshape(ts, s, d)
    h = x + 0.5 * gt_ref[...] * y
    if final:
        h = _rms_mod(h, gf_ref[...], shf_ref[...], scf_ref[...])
    o_ref[...] = h


def _ffn(x_prompt, x_sample, mod, j0, g, w1, w2, final_args=None):
    d = D_MODEL
    npt = x_prompt.shape[0] // TILE_STEPS
    in_specs = [_tok_spec(d, last=npt - 1), _tok_spec(d, last=0),
                _mod_spec(j0, npt), _mod_spec(j0 + 1, npt), _mod_spec(j0 + 2, npt),
                _const_spec((1, d)), _const_spec(w1.shape), _const_spec(w2.shape)]
    args = [x_prompt, x_sample, mod, mod, mod, g, w1, w2]
    if final_args is not None:
        g_final, modf = final_args
        in_specs += [_const_spec((1, d)), _mod_spec(0, npt), _mod_spec(1, npt)]
        args += [g_final, modf, modf]
    return pl.pallas_call(
        functools.partial(_ffn_body, final=final_args is not None, n_prompt_tiles=npt),
        grid=(npt + 1,), in_specs=in_specs, out_specs=_tok_spec(d),
        out_shape=jax.ShapeDtypeStruct((x_prompt.shape[0] + TILE_STEPS, SUBLANES, d), F32),
        compiler_params=_params(),
        name="ffn_final" if final_args is not None else "ffn",
    )(*args)


def _inproj_body(h_ref, sh_ref, sc_ref, g_ref, w_ref, b_ref, u_ref, v_ref, gates_ref):
    h = h_ref[...]
    ts, s, d = h.shape
    n = _rms_mod(h, g_ref[...], sh_ref[...], sc_ref[...])
    proj = jnp.dot(n.reshape(ts * s, d).astype(BF16), w_ref[...], preferred_element_type=F32) + b_ref[...]
    u_ref[...] = proj[:, :d].reshape(ts, s, d)
    v = proj[:, d:2 * d] * jax.nn.sigmoid(proj[:, 2 * d:3 * d])
    v_ref[...] = v.reshape(ts, s, d)
    gates_ref[...] = jax.nn.sigmoid(proj[:, 3 * d:]).reshape(ts, s, 2 * d)


def _inproj(h, mod, n_prompt_tiles, g, w_in, b_in):
    r, s, d = h.shape
    return pl.pallas_call(
        _inproj_body,
        grid=(r // TILE_STEPS,),
        in_specs=[_tok_spec(d), _mod_spec(3, n_prompt_tiles), _mod_spec(4, n_prompt_tiles),
                  _const_spec((1, d)), _const_spec(w_in.shape), _const_spec(b_in.shape)],
        out_specs=[_tok_spec(d), _tok_spec(d), _tok_spec(2 * d)],
        out_shape=[jax.ShapeDtypeStruct(h.shape, F32), jax.ShapeDtypeStruct(h.shape, F32),
                   jax.ShapeDtypeStruct((r, s, 2 * d), F32)],
        compiler_params=_params(),
        name="inproj",
    )(h, mod, mod, g, w_in, b_in)


_F_A1R, _F_A1I, _F_A2R, _F_A2I, _F_A4R, _F_A4I, _F_AR, _F_AI = range(8)


def _cmul(ar, ai, xr, xi):
    return ar * xr - ai * xi, ar * xi + ai * xr


def _scan_group(x_scr, ab_ref, fix_ref, cr_scr, ci_scr, kblock, group, steps, chained):
    blocks = range(group * SCAN_LANE_BLOCKS, (group + 1) * SCAN_LANE_BLOCKS)
    re_l = [pl.ds(j * LANES, LANES) for j in blocks]
    im_l = [pl.ds(STATE_PER_BLOCK + j * LANES, LANES) for j in blocks]
    c_l = [pl.ds(kblock * STATE_PER_BLOCK + j * LANES, LANES) for j in blocks]
    ar = [ab_ref[0, :, c] for c in c_l]
    ai = [ab_ref[1, :, c] for c in c_l]
    n = len(re_l)

    def rows(t):
        return pl.ds(pl.multiple_of(t * SUBLANES, SUBLANES), SUBLANES)

    def sweep(t, s):
        out = []
        for j in range(n):
            pr, pi = _cmul(ar[j], ai[j], s[2 * j], s[2 * j + 1])
            nr = pr + x_scr[rows(t), re_l[j]]
            ni = pi + x_scr[rows(t), im_l[j]]
            x_scr[rows(t), re_l[j]] = nr
            x_scr[rows(t), im_l[j]] = ni
            out += [nr, ni]
        return tuple(out)

    if not chained:
        s0 = []
        for j in range(n):
            s0 += [cr_scr[:, c_l[j]], ci_scr[:, c_l[j]]]
        s = jax.lax.fori_loop(0, steps, sweep, tuple(s0), unroll=True)
        for j in range(n):
            cr_scr[:, c_l[j]] = s[2 * j]
            ci_scr[:, c_l[j]] = s[2 * j + 1]
        return

    zero = jnp.zeros((SUBLANES, LANES), F32)
    ends = jax.lax.fori_loop(0, steps, sweep, (zero,) * (2 * n), unroll=True)
    sub = jax.lax.broadcasted_iota(jnp.int32, (SUBLANES, LANES), 0)
    starts = []
    for j in range(n):
        fx = [fix_ref[pl.ds(c * SUBLANES, SUBLANES), c_l[j]] for c in range(8)]
        er, ei = ends[2 * j], ends[2 * j + 1]
        kr = jnp.where(sub == 0, cr_scr[0:1, c_l[j]], pltpu.roll(er, 1, 0))
        ki = jnp.where(sub == 0, ci_scr[0:1, c_l[j]], pltpu.roll(ei, 1, 0))
        for d, (fr, fi) in ((1, (fx[_F_A1R], fx[_F_A1I])), (2, (fx[_F_A2R], fx[_F_A2I])),
                            (4, (fx[_F_A4R], fx[_F_A4I]))):
            pr, pi = _cmul(fr, fi, pltpu.roll(kr, d, 0), pltpu.roll(ki, d, 0))
            kr, ki = kr + pr, ki + pi
        pr, pi = _cmul(fx[_F_AR], fx[_F_AI], kr, ki)
        cr_scr[0:1, c_l[j]] = (pr + er)[SUBLANES - 1:SUBLANES, :]
        ci_scr[0:1, c_l[j]] = (pi + ei)[SUBLANES - 1:SUBLANES, :]
        starts += list(_cmul(ar[j], ai[j], kr, ki))

    def fixup(t, v):
        out = []
        for j in range(n):
            x_scr[rows(t), re_l[j]] = x_scr[rows(t), re_l[j]] + v[2 * j]
            x_scr[rows(t), im_l[j]] = x_scr[rows(t), im_l[j]] + v[2 * j + 1]
            out += list(_cmul(ar[j], ai[j], v[2 * j], v[2 * j + 1]))
        return tuple(out)

    jax.lax.fori_loop(0, steps, fixup, tuple(starts), unroll=True)


def _s5_body(u_ref, sr0_ref, si0_ref, bbd_ref, cbd_ref, ab_ref, fix_ref, dskip_ref,
             y_ref, sr_ref, si_ref, x_scr, cr_scr, ci_scr, *, chained):
    @pl.when(pl.program_id(0) == 0)
    def _():
        cr_scr[...] = sr0_ref[...]
        ci_scr[...] = si0_ref[...]

    ts, s, d = u_ref.shape
    u = u_ref[...].reshape(ts * s, d)
    ub = u.astype(BF16)
    def cols_of(k):
        return slice(k * MXU_DIM, (k + 1) * MXU_DIM)

    def bu(k):
        x_scr[k % 2] = jnp.dot(ub[:, cols_of(k)], bbd_ref[k], preferred_element_type=F32)

    bu(0)
    for k in range(N_KBLOCKS):
        cols = cols_of(k)
        xk = x_scr.at[k % 2]
        if k + 1 < N_KBLOCKS:
            bu(k + 1)
        for group in range(STATE_PER_BLOCK // LANES // SCAN_LANE_BLOCKS):
            _scan_group(xk, ab_ref, fix_ref, cr_scr, ci_scr, k, group, ts, chained)
        yk = jnp.dot(xk[...].astype(BF16), cbd_ref[k], preferred_element_type=F32)
        y_ref[:, :, cols] = (yk + dskip_ref[:, cols] * u[:, cols]).reshape(ts, s, MXU_DIM)
    sr_ref[...] = cr_scr[...]
    si_ref[...] = ci_scr[...]


def _s5(u, tile0, n_tiles, sr0, si0, sc, chained):
    s, d = SUBLANES, D_MODEL
    st = _const_spec((s, N_STATE))
    st_out = pl.BlockSpec((s, N_STATE), lambda i: (0, 0))
    return pl.pallas_call(
        functools.partial(_s5_body, chained=chained),
        grid=(n_tiles,),
        in_specs=[_tok_spec(d, tile0), st, st, _const_spec(sc['bbd'].shape), _const_spec(sc['cbd'].shape),
                  _const_spec(sc['ab'].shape), _const_spec(sc['fix'].shape), _const_spec(sc['dskip'].shape)],
        out_specs=[_tok_spec(d), st_out, st_out],
        out_shape=[jax.ShapeDtypeStruct((n_tiles * TILE_STEPS, s, d), F32),
                   jax.ShapeDtypeStruct((s, N_STATE), F32), jax.ShapeDtypeStruct((s, N_STATE), F32)],
        scratch_shapes=[pltpu.VMEM((2, TILE_STEPS * s, 2 * STATE_PER_BLOCK), F32),
                        pltpu.VMEM((s, N_STATE), F32), pltpu.VMEM((s, N_STATE), F32)],
        compiler_params=_params(),
        name="s5_chained" if chained else "s5",
    )(u, sr0, si0, sc['bbd'], sc['cbd'], sc['ab'], sc['fix'], sc['dskip'])


def _s5_constants(lam_re, lam_im, log_dt, b_re, b_im, c_re, c_im, d_skip, steps):
    dt = jnp.exp(log_dt)[:, None]
    mag = jnp.exp(lam_re * dt)
    ar = mag * jnp.cos(lam_im * dt)
    ai = mag * jnp.sin(lam_im * dt)
    den = lam_re * lam_re + lam_im * lam_im
    kr = ((ar - 1.0) * lam_re + ai * lam_im) / den
    ki = (ai * lam_re - (ar - 1.0) * lam_im) / den
    bbr = kr[..., None] * b_re - ki[..., None] * b_im
    bbi = kr[..., None] * b_im + ki[..., None] * b_re
    eye = jnp.eye(GROUPS_PER_BLOCK, dtype=F32)

    def b_blocks(bb):
        bb = bb.reshape(N_KBLOCKS, GROUPS_PER_BLOCK, SSM_STATE, SSM_GROUP)
        return jnp.einsum('kgpm,gh->kgmhp', bb, eye).reshape(N_KBLOCKS, MXU_DIM, STATE_PER_BLOCK)

    def c_blocks(cc):
        cc = cc.reshape(N_KBLOCKS, GROUPS_PER_BLOCK, SSM_GROUP, SSM_STATE)
        return jnp.einsum('kgmp,gh->kgphm', cc, eye).reshape(N_KBLOCKS, STATE_PER_BLOCK, MXU_DIM)

    bbd = jnp.concatenate([b_blocks(bbr), b_blocks(bbi)], axis=2).astype(BF16)
    cbd = jnp.concatenate([c_blocks(c_re), -c_blocks(c_im)], axis=1).astype(BF16)

    a = (ar.reshape(1, N_STATE), ai.reshape(1, N_STATE))
    ab = jnp.stack([jnp.broadcast_to(a[0], (SUBLANES, N_STATE)), jnp.broadcast_to(a[1], (SUBLANES, N_STATE))])

    def csq(x):
        return _cmul(x[0], x[1], x[0], x[1])

    assert steps & (steps - 1) == 0, "a^steps by repeated squaring"
    big = a
    for _ in range(steps.bit_length() - 1):
        big = csq(big)
    big2 = csq(big)
    big4 = csq(big2)
    sub = jnp.arange(SUBLANES, dtype=jnp.int32)[:, None]
    slabs = []
    for d, pw in ((1, big), (2, big2), (4, big4)):
        slabs += [jnp.where(sub >= d, pw[0], 0.0), jnp.where(sub >= d, pw[1], 0.0)]
    slabs += [jnp.broadcast_to(big[0], (SUBLANES, N_STATE)), jnp.broadcast_to(big[1], (SUBLANES, N_STATE))]
    fix = jnp.concatenate(slabs, axis=0)
    return dict(bbd=bbd, cbd=cbd, ab=ab, fix=fix, dskip=d_skip.reshape(1, D_MODEL))


def _depthwise_conv(ext_scr, wdw_ref, bdw_ref, conv_scr, steps):
    chunks = steps // CONV_CHUNK

    for k0 in range(0, CONV_WIDTH, CONV_TAPS):
        taps = range(k0, min(k0 + CONV_TAPS, CONV_WIDTH))

        def unit(it, carry, k0=k0, taps=taps):
            j = it // chunks
            t0 = (it % chunks) * CONV_CHUNK
            w = {k: wdw_ref[j, k] for k in taps}
            if k0 == 0:
                acc = [bdw_ref[j]] * CONV_CHUNK
            else:
                acc = [conv_scr[j, t0 + t] for t in range(CONV_CHUNK)]
            for e in range(taps[0], CONV_CHUNK + taps[-1]):
                x = ext_scr[j, t0 + e]
                for t in range(CONV_CHUNK):
                    if e - t in w:
                        acc[t] = acc[t] + w[e - t] * x
            for t in range(CONV_CHUNK):
                conv_scr[j, t0 + t] = acc[t]
            return carry

        jax.lax.fori_loop(0, (D_MODEL // LANES) * chunks, unit, 0)


def _mix_body(h_ref, ys_ref, v_ref, hist0_ref, gates_ref, gt_ref,
              wglu_ref, wdw_ref, bdw_ref, lng_ref, lnb_ref, wpw_ref, bpw_ref, wout_ref,
              o_ref, ext_scr, tail_scr, conv_scr, *, chained):
    ts, s, d = h_ref.shape
    m = ts * s

    @pl.when(pl.program_id(0) == 0)
    def _():
        tail_scr[...] = hist0_ref[...]

    cur_tail = v_ref[pl.ds(ts - HIST, HIST), :, :]
    if chained:
        sub = jax.lax.broadcasted_iota(jnp.int32, (HIST, s, d), 1)
        hist = jnp.where(sub == 0, pltpu.roll(tail_scr[...], 1, 1), pltpu.roll(cur_tail, 1, 1))
    else:
        hist = tail_scr[...]
    for j in range(d // LANES):
        ext_scr[j, pl.ds(0, HIST)] = hist[:, :, j * LANES:(j + 1) * LANES]
        ext_scr[j, pl.ds(HIST, ts)] = v_ref[:, :, j * LANES:(j + 1) * LANES]
    tail_scr[...] = cur_tail
    _depthwise_conv(ext_scr, wdw_ref, bdw_ref, conv_scr, ts)

    conv = jnp.concatenate([conv_scr[j] for j in range(d // LANES)], axis=-1)
    mu = jnp.mean(conv, axis=-1, keepdims=True)
    cen = conv - mu
    var = jnp.mean(cen * cen, axis=-1, keepdims=True)
    ln = cen * jax.lax.rsqrt(var + EPS) * lng_ref[...] + lnb_ref[...]
    vb = jax.nn.silu(ln).reshape(m, d).astype(BF16)
    y_b = jnp.dot(vb, wpw_ref[...], preferred_element_type=F32) + bpw_ref[...]
    ga = jax.nn.gelu(ys_ref[...].reshape(m, d)).astype(BF16)
    ag = jnp.dot(ga, wglu_ref[...], preferred_element_type=F32)
    y_a = ag[:, :d] * jax.nn.sigmoid(ag[:, d:])
    gates = gates_ref[...].reshape(m, 2 * d)
    merged = (gates[:, :d] * y_a + gates[:, d:] * y_b).astype(BF16)
    out = jnp.dot(merged, wout_ref[...], preferred_element_type=F32).reshape(ts, s, d)
    o_ref[...] = h_ref[...] + gt_ref[...] * out


def _mix(h, ys, v, gates, mod, group, tile0, hist0, p, chained):
    s, d = SUBLANES, D_MODEL
    n_tiles = ys.shape[0] // TILE_STEPS
    return pl.pallas_call(
        functools.partial(_mix_body, chained=chained),
        grid=(n_tiles,),
        in_specs=[_tok_spec(d, tile0), _tok_spec(d), _tok_spec(d, tile0), _const_spec(hist0.shape),
                  _tok_spec(2 * d, tile0), _mod_spec(5, group=group),
                  _const_spec(p['w_glu'].shape), _const_spec(p['w_dw'].shape), _const_spec(p['b_dw'].shape),
                  _const_spec((1, d)), _const_spec((1, d)), _const_spec(p['w_pw'].shape),
                  _const_spec((1, d)), _const_spec(p['w_out'].shape)],
        out_specs=_tok_spec(d),
        out_shape=jax.ShapeDtypeStruct(ys.shape, F32),
        scratch_shapes=[pltpu.VMEM((d // LANES, TILE_STEPS + HIST, s, LANES), F32), pltpu.VMEM((HIST, s, d), F32),
                        pltpu.VMEM((d // LANES, TILE_STEPS, s, LANES), F32)],
        compiler_params=_params(),
        name="mix_chained" if chained else "mix",
    )(h, ys, v, hist0, gates, mod, p['w_glu'], p['w_dw'], p['b_dw'], p['ln_g'], p['ln_b'],
      p['w_pw'], p['b_pw'], p['w_out'])


def kernel(x_prompt, x_sample, c_prompt, c_sample, state_ssm_re, state_ssm_im, cache_conv, w_ada, b_ada, g_ffn1, w1_ffn1, w2_ffn1, g_mix, w_in, b_in, lam_re, lam_im, log_dt, b_re, b_im, c_re, c_im, d_skip, w_glu, w_dw, b_dw, ln_g, ln_b, w_pw, b_pw, w_out, g_ffn2, w1_ffn2, w2_ffn2, g_final, w_ada_f, b_ada_f):
    d, s = D_MODEL, SUBLANES
    bp, lp = x_prompt.shape[:2]
    bs, ls = x_sample.shape[:2]
    assert w_ada.shape[0] == 1 and bp == 1 and bs == s, "single layer, one prompt, 8 sample sequences"
    tile = s * TILE_STEPS
    assert lp % tile == 0 and ls == TILE_STEPS and TILE_STEPS >= HIST
    row = lambda a: a.reshape(1, -1)
    nlb = d // LANES
    w_dw_blocks = w_dw[0].reshape(CONV_WIDTH, nlb, 1, LANES).transpose(1, 0, 2, 3)
    p = dict(
        w_glu=w_glu[0].astype(BF16), w_dw=jnp.broadcast_to(w_dw_blocks, (nlb, CONV_WIDTH, s, LANES)),
        b_dw=jnp.broadcast_to(b_dw[0].reshape(nlb, 1, LANES), (nlb, s, LANES)),
        ln_g=row(ln_g[0]), ln_b=row(ln_b[0]),
        w_pw=w_pw[0].astype(BF16), b_pw=row(b_pw[0]), w_out=w_out[0].astype(BF16),
    )
    sc = _s5_constants(lam_re[0], lam_im[0], log_dt[0], b_re[0], b_im[0], c_re[0], c_im[0], d_skip[0], TILE_STEPS)

    c_all = jnp.concatenate([c_sample, jnp.broadcast_to(c_prompt, (s, d))], axis=0)
    mod = _ada(c_all, w_ada[0], row(b_ada[0]))
    modf = _ada(c_all, w_ada_f, row(b_ada_f))

    npt = lp // tile
    rp = npt * TILE_STEPS
    xp = x_prompt.reshape(npt, s, TILE_STEPS, d).transpose(0, 2, 1, 3).reshape(rp, s, d)
    xs = x_sample.transpose(1, 0, 2)

    h1 = _ffn(xp, xs, mod, 0, row(g_ffn1[0]), w1_ffn1[0].astype(BF16), w2_ffn1[0].astype(BF16))
    u, v, gates = _inproj(h1, mod, npt, row(g_mix[0]), w_in[0].astype(BF16), row(b_in[0]))

    zs = jnp.zeros((s, N_STATE), F32)
    ys_p, re_p, im_p = _s5(u, 0, npt, zs, zs, sc, chained=True)
    ys_s, re_s, im_s = _s5(u, npt, 1, state_ssm_re[0].reshape(bs, N_STATE), state_ssm_im[0].reshape(bs, N_STATE),
                           sc, chained=False)
    h2_p = _mix(h1, ys_p, v, gates, mod, PROMPT_GROUP, 0, jnp.zeros((HIST, s, d), F32), p, chained=True)
    h2_s = _mix(h1, ys_s, v, gates, mod, SAMPLE_GROUP, npt, cache_conv[0].transpose(1, 0, 2), p, chained=False)

    y = _ffn(h2_p, h2_s, mod, 6, row(g_ffn2[0]), w1_ffn2[0].astype(BF16), w2_ffn2[0].astype(BF16),
             final_args=(row(g_final), modf))

    y_prompt = y[:rp].reshape(npt, TILE_STEPS, s, d).transpose(0, 2, 1, 3).reshape(bp, lp, d)
    y_sample = y[rp:].transpose(1, 0, 2)
    conv_p = v[rp - HIST:rp, s - 1, :].reshape(1, bp, HIST, d)
    conv_s = v[rp + TILE_STEPS - HIST:].transpose(1, 0, 2)[None]
    st = lambda a, n: a[:n].reshape(1, n, N_GROUPS, SSM_STATE)
    return (y_prompt, y_sample, st(re_p, bp), st(im_p, bp), conv_p, st(re_s, bs), st(im_s, bs), conv_s)
```

```python
import functools

import jax
import jax.numpy as jnp
from jax.experimental import pallas as pl
from jax.experimental.pallas import tpu as pltpu

D_MODEL = 1024
N_GROUPS = 64
SSM_GROUP = 16
SSM_STATE = 64
CONV_WIDTH = 31
D_FF = 2816
N_MOD = 9
EPS = 1e-6

SUBLANES = 8
LANES = 128
MXU_DIM = 256
VMEM_LIMIT_BYTES = 56 * 1024 * 1024

GROUPS_PER_BLOCK = MXU_DIM // SSM_GROUP
N_KBLOCKS = N_GROUPS // GROUPS_PER_BLOCK
STATE_PER_BLOCK = GROUPS_PER_BLOCK * SSM_STATE
N_STATE = N_GROUPS * SSM_STATE
HIST = CONV_WIDTH - 1
TILE_STEPS = 64
SCAN_LANE_BLOCKS = 4
CONV_CHUNK = 16
CONV_TAPS = 16
SAMPLE_GROUP, PROMPT_GROUP = 0, 1

BF16 = jnp.bfloat16
F32 = jnp.float32


def _const_spec(shape):
    return pl.BlockSpec(shape, lambda *_: (0,) * len(shape), pipeline_mode=pl.Buffered(1))


def _params():
    return pltpu.CompilerParams(dimension_semantics=("arbitrary",), vmem_limit_bytes=VMEM_LIMIT_BYTES)


def _rms_mod(x, g, shift, scale):
    y = x * jax.lax.rsqrt(jnp.mean(x * x, axis=-1, keepdims=True) + EPS) * g
    return y * (1.0 + scale) + shift


def _tok_spec(width, tile0=0, last=None):
    if last is None:
        return pl.BlockSpec((TILE_STEPS, SUBLANES, width), lambda i: (i + tile0, 0, 0))
    return pl.BlockSpec((TILE_STEPS, SUBLANES, width), lambda i: (jnp.minimum(i + tile0, last), 0, 0))


def _mod_spec(j, n_prompt_tiles=None, group=None):
    if group is not None:
        return pl.BlockSpec((SUBLANES, D_MODEL), lambda i: (group, j))
    return pl.BlockSpec((SUBLANES, D_MODEL),
                        lambda i: (jnp.where(i < n_prompt_tiles, PROMPT_GROUP, SAMPLE_GROUP), j))


def _ada_body(c_ref, w_ref, b_ref, o_ref):
    a = jax.nn.silu(c_ref[...]).astype(BF16)
    o_ref[...] = jnp.dot(a, w_ref[...].astype(BF16), preferred_element_type=F32) + b_ref[...]


def _ada(c, w, b):
    r, n = c.shape[0], w.shape[1]
    tn = 1024
    return pl.pallas_call(
        _ada_body,
        grid=(n // tn,),
        in_specs=[pl.BlockSpec((r, D_MODEL), lambda j: (0, 0)),
                  pl.BlockSpec((D_MODEL, tn), lambda j: (0, j)),
                  pl.BlockSpec((1, tn), lambda j: (0, j))],
        out_specs=pl.BlockSpec((r, tn), lambda j: (0, j)),
        out_shape=jax.ShapeDtypeStruct((r, n), F32),
        compiler_params=_params(),
        name="ada",
    )(c, w, b)


def _rows_to_slabs(x_ref, stage_scr):
    for j in range(D_MODEL // LANES):
        for c in range(SUBLANES):
            stage_scr[j, pl.ds(c, TILE_STEPS, stride=SUBLANES), :] = (
                x_ref[0, pl.ds(c * TILE_STEPS, TILE_STEPS), pl.ds(j * LANES, LANES)])


def _staged_slabs(stage_scr):
    return jnp.concatenate([stage_scr[j].reshape(TILE_STEPS, SUBLANES, LANES)
                            for j in range(D_MODEL // LANES)], axis=-1)


def _slabs_to_rows(h, stage_scr, o_ref):
    for j in range(D_MODEL // LANES):
        stage_scr[j] = h[:, :, j * LANES:(j + 1) * LANES].reshape(TILE_STEPS * SUBLANES, LANES)
    for j in range(D_MODEL // LANES):
        for c in range(SUBLANES):
            o_ref[0, pl.ds(c * TILE_STEPS, TILE_STEPS), pl.ds(j * LANES, LANES)] = (
                stage_scr[j, pl.ds(c, TILE_STEPS, stride=SUBLANES), :])


def _swiglu_residual(x, sh_ref, sc_ref, gt_ref, g_ref, w1_ref, w2_ref):
    ts, s, d = x.shape
    n = _rms_mod(x, g_ref[...], sh_ref[...], sc_ref[...])
    gu = jnp.dot(n.reshape(ts * s, d).astype(BF16), w1_ref[...], preferred_element_type=F32)
    a = (jax.nn.silu(gu[:, :D_FF]) * gu[:, D_FF:]).astype(BF16)
    y = jnp.dot(a, w2_ref[...], preferred_element_type=F32).reshape(ts, s, d)
    return x + 0.5 * gt_ref[...] * y


def _ffn_in_body(xp_ref, xs_ref, sh_ref, sc_ref, gt_ref, g_ref, w1_ref, w2_ref, o_ref, stage_scr,
                 *, n_prompt_tiles):
    i = pl.program_id(0)

    @pl.when(i < n_prompt_tiles)
    def _():
        _rows_to_slabs(xp_ref, stage_scr)

    @pl.when(i >= n_prompt_tiles)
    def _():
        _rows_to_slabs(xs_ref, stage_scr)

    o_ref[...] = _swiglu_residual(_staged_slabs(stage_scr), sh_ref, sc_ref, gt_ref, g_ref, w1_ref, w2_ref)


def _ffn_in(x_prompt, x_sample, mod, g, w1, w2):
    d = D_MODEL
    tile = SUBLANES * TILE_STEPS
    npt = x_prompt.shape[1] // tile
    return pl.pallas_call(
        functools.partial(_ffn_in_body, n_prompt_tiles=npt),
        grid=(npt + 1,),
        in_specs=[pl.BlockSpec((1, tile, d), lambda i: (0, jnp.minimum(i, npt - 1), 0)),
                  pl.BlockSpec((1, tile, d), lambda i: (0, 0, 0)),
                  _mod_spec(0, npt), _mod_spec(1, npt), _mod_spec(2, npt),
                  _const_spec((1, d)), _const_spec(w1.shape), _const_spec(w2.shape)],
        out_specs=_tok_spec(d),
        out_shape=jax.ShapeDtypeStruct(((npt + 1) * TILE_STEPS, SUBLANES, d), F32),
        scratch_shapes=[pltpu.VMEM((d // LANES, tile, LANES), F32)],
        compiler_params=_params(),
        name="ffn_in",
    )(x_prompt, x_sample, mod, mod, mod, g, w1, w2)


def _ffn_out_body(x_ref, sh_ref, sc_ref, gt_ref, g_ref, w1_ref, w2_ref, gf_ref, shf_ref, scf_ref, o_ref, stage_scr):
    h = _swiglu_residual(x_ref[...], sh_ref, sc_ref, gt_ref, g_ref, w1_ref, w2_ref)
    _slabs_to_rows(_rms_mod(h, gf_ref[...], shf_ref[...], scf_ref[...]), stage_scr, o_ref)


def _ffn_out(x, mod, modf, group, g, w1, w2, g_final):
    d = D_MODEL
    tile = SUBLANES * TILE_STEPS
    n_tiles = x.shape[0] // TILE_STEPS
    return pl.pallas_call(
        _ffn_out_body,
        grid=(n_tiles,),
        in_specs=[_tok_spec(d), _mod_spec(6, group=group), _mod_spec(7, group=group), _mod_spec(8, group=group),
                  _const_spec((1, d)), _const_spec(w1.shape), _const_spec(w2.shape),
                  _const_spec((1, d)), _mod_spec(0, group=group), _mod_spec(1, group=group)],
        out_specs=pl.BlockSpec((1, tile, d), lambda i: (0, i, 0)),
        out_shape=jax.ShapeDtypeStruct((1, n_tiles * tile, d), F32),
        scratch_shapes=[pltpu.VMEM((d // LANES, tile, LANES), F32)],
        compiler_params=_params(),
        name="ffn_out",
    )(x, mod, mod, mod, g, w1, w2, g_final, modf, modf)


def _inproj_body(h_ref, sh_ref, sc_ref, g_ref, w_ref, b_ref, u_ref, v_ref, gates_ref):
    h = h_ref[...]
    ts, s, d = h.shape
    n = _rms_mod(h, g_ref[...], sh_ref[...], sc_ref[...])
    proj = jnp.dot(n.reshape(ts * s, d).astype(BF16), w_ref[...], preferred_element_type=F32) + b_ref[...]
    u_ref[...] = proj[:, :d].reshape(ts, s, d)
    v = proj[:, d:2 * d] * jax.nn.sigmoid(proj[:, 2 * d:3 * d])
    v_ref[...] = v.reshape(ts, s, d)
    gates_ref[...] = jax.nn.sigmoid(proj[:, 3 * d:]).reshape(ts, s, 2 * d)


def _inproj(h, mod, n_prompt_tiles, g, w_in, b_in):
    r, s, d = h.shape
    return pl.pallas_call(
        _inproj_body,
        grid=(r // TILE_STEPS,),
        in_specs=[_tok_spec(d), _mod_spec(3, n_prompt_tiles), _mod_spec(4, n_prompt_tiles),
                  _const_spec((1, d)), _const_spec(w_in.shape), _const_spec(b_in.shape)],
        out_specs=[_tok_spec(d), _tok_spec(d), _tok_spec(2 * d)],
        out_shape=[jax.ShapeDtypeStruct(h.shape, F32), jax.ShapeDtypeStruct(h.shape, F32),
                   jax.ShapeDtypeStruct((r, s, 2 * d), F32)],
        compiler_params=_params(),
        name="inproj",
    )(h, mod, mod, g, w_in, b_in)


_F_A1R, _F_A1I, _F_A2R, _F_A2I, _F_A4R, _F_A4I, _F_AR, _F_AI = range(8)


def _cmul(ar, ai, xr, xi):
    return ar * xr - ai * xi, ar * xi + ai * xr


def _scan_group(x_scr, ab_ref, fix_ref, cr_scr, ci_scr, kblock, group, steps, chained):
    blocks = range(group * SCAN_LANE_BLOCKS, (group + 1) * SCAN_LANE_BLOCKS)
    re_l = [pl.ds(j * LANES, LANES) for j in blocks]
    im_l = [pl.ds(STATE_PER_BLOCK + j * LANES, LANES) for j in blocks]
    c_l = [pl.ds(kblock * STATE_PER_BLOCK + j * LANES, LANES) for j in blocks]
    ar = [ab_ref[0, :, c] for c in c_l]
    ai = [ab_ref[1, :, c] for c in c_l]
    n = len(re_l)

    def rows(t):
        return pl.ds(pl.multiple_of(t * SUBLANES, SUBLANES), SUBLANES)

    def sweep(t, s):
        out = []
        for j in range(n):
            pr, pi = _cmul(ar[j], ai[j], s[2 * j], s[2 * j + 1])
            nr = pr + x_scr[rows(t), re_l[j]]
            ni = pi + x_scr[rows(t), im_l[j]]
            x_scr[rows(t), re_l[j]] = nr
            x_scr[rows(t), im_l[j]] = ni
            out += [nr, ni]
        return tuple(out)

    if not chained:
        s0 = []
        for j in range(n):
            s0 += [cr_scr[:, c_l[j]], ci_scr[:, c_l[j]]]
        s = jax.lax.fori_loop(0, steps, sweep, tuple(s0), unroll=True)
        for j in range(n):
            cr_scr[:, c_l[j]] = s[2 * j]
            ci_scr[:, c_l[j]] = s[2 * j + 1]
        return

    zero = jnp.zeros((SUBLANES, LANES), F32)
    ends = jax.lax.fori_loop(0, steps, sweep, (zero,) * (2 * n), unroll=True)
    sub = jax.lax.broadcasted_iota(jnp.int32, (SUBLANES, LANES), 0)
    starts = []
    for j in range(n):
        fx = [fix_ref[pl.ds(c * SUBLANES, SUBLANES), c_l[j]] for c in range(8)]
        er, ei = ends[2 * j], ends[2 * j + 1]
        kr = jnp.where(sub == 0, cr_scr[0:1, c_l[j]], pltpu.roll(er, 1, 0))
        ki = jnp.where(sub == 0, ci_scr[0:1, c_l[j]], pltpu.roll(ei, 1, 0))
        for d, (fr, fi) in ((1, (fx[_F_A1R], fx[_F_A1I])), (2, (fx[_F_A2R], fx[_F_A2I])),
                            (4, (fx[_F_A4R], fx[_F_A4I]))):
            pr, pi = _cmul(fr, fi, pltpu.roll(kr, d, 0), pltpu.roll(ki, d, 0))
            kr, ki = kr + pr, ki + pi
        pr, pi = _cmul(fx[_F_AR], fx[_F_AI], kr, ki)
        cr_scr[0:1, c_l[j]] = (pr + er)[SUBLANES - 1:SUBLANES, :]
        ci_scr[0:1, c_l[j]] = (pi + ei)[SUBLANES - 1:SUBLANES, :]
        starts += list(_cmul(ar[j], ai[j], kr, ki))

    def fixup(t, v):
        out = []
        for j in range(n):
            x_scr[rows(t), re_l[j]] = x_scr[rows(t), re_l[j]] + v[2 * j]
            x_scr[rows(t), im_l[j]] = x_scr[rows(t), im_l[j]] + v[2 * j + 1]
            out += list(_cmul(ar[j], ai[j], v[2 * j], v[2 * j + 1]))
        return tuple(out)

    jax.lax.fori_loop(0, steps, fixup, tuple(starts), unroll=True)


def _s5_body(u_ref, sr0_ref, si0_ref, bbd_ref, cbd_ref, ab_ref, fix_ref, dskip_ref,
             y_ref, sr_ref, si_ref, x_scr, cr_scr, ci_scr, *, chained):
    @pl.when(pl.program_id(0) == 0)
    def _():
        cr_scr[...] = sr0_ref[...]
        ci_scr[...] = si0_ref[...]

    ts, s, d = u_ref.shape
    u = u_ref[...].reshape(ts * s, d)
    ub = u.astype(BF16)
    def cols_of(k):
        return slice(k * MXU_DIM, (k + 1) * MXU_DIM)

    def bu(k):
        x_scr[k % 2] = jnp.dot(ub[:, cols_of(k)], bbd_ref[k], preferred_element_type=F32)

    bu(0)
    for k in range(N_KBLOCKS):
        cols = cols_of(k)
        xk = x_scr.at[k % 2]
        if k + 1 < N_KBLOCKS:
            bu(k + 1)
        for group in range(STATE_PER_BLOCK // LANES // SCAN_LANE_BLOCKS):
            _scan_group(xk, ab_ref, fix_ref, cr_scr, ci_scr, k, group, ts, chained)
        yk = jnp.dot(xk[...].astype(BF16), cbd_ref[k], preferred_element_type=F32)
        y_ref[:, :, cols] = (yk + dskip_ref[:, cols] * u[:, cols]).reshape(ts, s, MXU_DIM)
    sr_ref[...] = cr_scr[...]
    si_ref[...] = ci_scr[...]


def _s5(u, tile0, n_tiles, sr0, si0, sc, chained):
    s, d = SUBLANES, D_MODEL
    st = _const_spec((s, N_STATE))
    st_out = pl.BlockSpec((s, N_STATE), lambda i: (0, 0))
    return pl.pallas_call(
        functools.partial(_s5_body, chained=chained),
        grid=(n_tiles,),
        in_specs=[_tok_spec(d, tile0), st, st, _const_spec(sc['bbd'].shape), _const_spec(sc['cbd'].shape),
                  _const_spec(sc['ab'].shape), _const_spec(sc['fix'].shape), _const_spec(sc['dskip'].shape)],
        out_specs=[_tok_spec(d), st_out, st_out],
        out_shape=[jax.ShapeDtypeStruct((n_tiles * TILE_STEPS, s, d), F32),
                   jax.ShapeDtypeStruct((s, N_STATE), F32), jax.ShapeDtypeStruct((s, N_STATE), F32)],
        scratch_shapes=[pltpu.VMEM((2, TILE_STEPS * s, 2 * STATE_PER_BLOCK), F32),
                        pltpu.VMEM((s, N_STATE), F32), pltpu.VMEM((s, N_STATE), F32)],
        compiler_params=_params(),
        name="s5_chained" if chained else "s5",
    )(u, sr0, si0, sc['bbd'], sc['cbd'], sc['ab'], sc['fix'], sc['dskip'])


def _s5_constants(lam_re, lam_im, log_dt, b_re, b_im, c_re, c_im, d_skip, steps):
    dt = jnp.exp(log_dt)[:, None]
    mag = jnp.exp(lam_re * dt)
    ar = mag * jnp.cos(lam_im * dt)
    ai = mag * jnp.sin(lam_im * dt)
    den = lam_re * lam_re + lam_im * lam_im
    kr = ((ar - 1.0) * lam_re + ai * lam_im) / den
    ki = (ai * lam_re - (ar - 1.0) * lam_im) / den
    bbr = kr[..., None] * b_re - ki[..., None] * b_im
    bbi = kr[..., None] * b_im + ki[..., None] * b_re
    eye = jnp.eye(GROUPS_PER_BLOCK, dtype=F32)

    def b_blocks(bb):
        bb = bb.reshape(N_KBLOCKS, GROUPS_PER_BLOCK, SSM_STATE, SSM_GROUP)
        return jnp.einsum('kgpm,gh->kgmhp', bb, eye).reshape(N_KBLOCKS, MXU_DIM, STATE_PER_BLOCK)

    def c_blocks(cc):
        cc = cc.reshape(N_KBLOCKS, GROUPS_PER_BLOCK, SSM_GROUP, SSM_STATE)
        return jnp.einsum('kgmp,gh->kgphm', cc, eye).reshape(N_KBLOCKS, STATE_PER_BLOCK, MXU_DIM)

    bbd = jnp.concatenate([b_blocks(bbr), b_blocks(bbi)], axis=2).astype(BF16)
    cbd = jnp.concatenate([c_blocks(c_re), -c_blocks(c_im)], axis=1).astype(BF16)

    a = (ar.reshape(1, N_STATE), ai.reshape(1, N_STATE))
    ab = jnp.stack([jnp.broadcast_to(a[0], (SUBLANES, N_STATE)), jnp.broadcast_to(a[1], (SUBLANES, N_STATE))])

    def csq(x):
        return _cmul(x[0], x[1], x[0], x[1])

    assert steps & (steps - 1) == 0, "a^steps by repeated squaring"
    big = a
    for _ in range(steps.bit_length() - 1):
        big = csq(big)
    big2 = csq(big)
    big4 = csq(big2)
    sub = jnp.arange(SUBLANES, dtype=jnp.int32)[:, None]
    slabs = []
    for d, pw in ((1, big), (2, big2), (4, big4)):
        slabs += [jnp.where(sub >= d, pw[0], 0.0), jnp.where(sub >= d, pw[1], 0.0)]
    slabs += [jnp.broadcast_to(big[0], (SUBLANES, N_STATE)), jnp.broadcast_to(big[1], (SUBLANES, N_STATE))]
    fix = jnp.concatenate(slabs, axis=0)
    return dict(bbd=bbd, cbd=cbd, ab=ab, fix=fix, dskip=d_skip.reshape(1, D_MODEL))


def _depthwise_conv(ext_scr, wdw_ref, bdw_ref, conv_scr, steps):
    chunks = steps // CONV_CHUNK

    for k0 in range(0, CONV_WIDTH, CONV_TAPS):
        taps = range(k0, min(k0 + CONV_TAPS, CONV_WIDTH))

        def unit(it, carry, k0=k0, taps=taps):
            j = it // chunks
            t0 = (it % chunks) * CONV_CHUNK
            w = {k: wdw_ref[j, k] for k in taps}
            if k0 == 0:
                acc = [bdw_ref[j]] * CONV_CHUNK
            else:
                acc = [conv_scr[j, t0 + t] for t in range(CONV_CHUNK)]
            for e in range(taps[0], CONV_CHUNK + taps[-1]):
                x = ext_scr[j, t0 + e]
                for t in range(CONV_CHUNK):
                    if e - t in w:
                        acc[t] = acc[t] + w[e - t] * x
            for t in range(CONV_CHUNK):
                conv_scr[j, t0 + t] = acc[t]
            return carry

        jax.lax.fori_loop(0, (D_MODEL // LANES) * chunks, unit, 0)


def _mix_body(h_ref, ys_ref, v_ref, hist0_ref, gates_ref, gt_ref,
              wglu_ref, wdw_ref, bdw_ref, lng_ref, lnb_ref, wpw_ref, bpw_ref, wout_ref,
              o_ref, ext_scr, tail_scr, conv_scr, *, chained):
    ts, s, d = h_ref.shape
    m = ts * s

    @pl.when(pl.program_id(0) == 0)
    def _():
        tail_scr[...] = hist0_ref[...]

    cur_tail = v_ref[pl.ds(ts - HIST, HIST), :, :]
    if chained:
        sub = jax.lax.broadcasted_iota(jnp.int32, (HIST, s, d), 1)
        hist = jnp.where(sub == 0, pltpu.roll(tail_scr[...], 1, 1), pltpu.roll(cur_tail, 1, 1))
    else:
        hist = tail_scr[...]
    for j in range(d // LANES):
        ext_scr[j, pl.ds(0, HIST)] = hist[:, :, j * LANES:(j + 1) * LANES]
        ext_scr[j, pl.ds(HIST, ts)] = v_ref[:, :, j * LANES:(j + 1) * LANES]
    tail_scr[...] = cur_tail
    _depthwise_conv(ext_scr, wdw_ref, bdw_ref, conv_scr, ts)

    conv = jnp.concatenate([conv_scr[j] for j in range(d // LANES)], axis=-1)
    mu = jnp.mean(conv, axis=-1, keepdims=True)
    cen = conv - mu
    var = jnp.mean(cen * cen, axis=-1, keepdims=True)
    ln = cen * jax.lax.rsqrt(var + EPS) * lng_ref[...] + lnb_ref[...]
    vb = jax.nn.silu(ln).reshape(m, d).astype(BF16)
    y_b = jnp.dot(vb, wpw_ref[...], preferred_element_type=F32) + bpw_ref[...]
    ga = jax.nn.gelu(ys_ref[...].reshape(m, d)).astype(BF16)
    ag = jnp.dot(ga, wglu_ref[...], preferred_element_type=F32)
    y_a = ag[:, :d] * jax.nn.sigmoid(ag[:, d:])
    gates = gates_ref[...].reshape(m, 2 * d)
    merged = (gates[:, :d] * y_a + gates[:, d:] * y_b).astype(BF16)
    out = jnp.dot(merged, wout_ref[...], preferred_element_type=F32).reshape(ts, s, d)
    o_ref[...] = h_ref[...] + gt_ref[...] * out


def _mix(h, ys, v, gates, mod, group, tile0, hist0, p, chained):
    s, d = SUBLANES, D_MODEL
    n_tiles = ys.shape[0] // TILE_STEPS
    return pl.pallas_call(
        functools.partial(_mix_body, chained=chained),
        grid=(n_tiles,),
        in_specs=[_tok_spec(d, tile0), _tok_spec(d), _tok_spec(d, tile0), _const_spec(hist0.shape),
                  _tok_spec(2 * d, tile0), _mod_spec(5, group=group),
                  _const_spec(p['w_glu'].shape), _const_spec(p['w_dw'].shape), _const_spec(p['b_dw'].shape),
                  _const_spec((1, d)), _const_spec((1, d)), _const_spec(p['w_pw'].shape),
                  _const_spec((1, d)), _const_spec(p['w_out'].shape)],
        out_specs=_tok_spec(d),
        out_shape=jax.ShapeDtypeStruct(ys.shape, F32),
        scratch_shapes=[pltpu.VMEM((d // LANES, TILE_STEPS + HIST, s, LANES), F32), pltpu.VMEM((HIST, s, d), F32),
                        pltpu.VMEM((d // LANES, TILE_STEPS, s, LANES), F32)],
        compiler_params=_params(),
        name="mix_chained" if chained else "mix",
    )(h, ys, v, hist0, gates, mod, p['w_glu'], p['w_dw'], p['b_dw'], p['ln_g'], p['ln_b'],
      p['w_pw'], p['b_pw'], p['w_out'])


def kernel(x_prompt, x_sample, c_prompt, c_sample, state_ssm_re, state_ssm_im, cache_conv, w_ada, b_ada, g_ffn1, w1_ffn1, w2_ffn1, g_mix, w_in, b_in, lam_re, lam_im, log_dt, b_re, b_im, c_re, c_im, d_skip, w_glu, w_dw, b_dw, ln_g, ln_b, w_pw, b_pw, w_out, g_ffn2, w1_ffn2, w2_ffn2, g_final, w_ada_f, b_ada_f):
    d, s = D_MODEL, SUBLANES
    bp, lp = x_prompt.shape[:2]
    bs, ls = x_sample.shape[:2]
    assert w_ada.shape[0] == 1 and bp == 1 and bs == s, "single layer, one prompt, 8 sample sequences"
    tile = s * TILE_STEPS
    assert lp % tile == 0 and ls == TILE_STEPS and TILE_STEPS >= HIST
    row = lambda a: a.reshape(1, -1)
    nlb = d // LANES
    w_dw_blocks = w_dw[0].reshape(CONV_WIDTH, nlb, 1, LANES).transpose(1, 0, 2, 3)
    p = dict(
        w_glu=w_glu[0].astype(BF16), w_dw=jnp.broadcast_to(w_dw_blocks, (nlb, CONV_WIDTH, s, LANES)),
        b_dw=jnp.broadcast_to(b_dw[0].reshape(nlb, 1, LANES), (nlb, s, LANES)),
        ln_g=row(ln_g[0]), ln_b=row(ln_b[0]),
        w_pw=w_pw[0].astype(BF16), b_pw=row(b_pw[0]), w_out=w_out[0].astype(BF16),
    )
    sc = _s5_constants(lam_re[0], lam_im[0], log_dt[0], b_re[0], b_im[0], c_re[0], c_im[0], d_skip[0], TILE_STEPS)

    c_all = jnp.concatenate([c_sample, jnp.broadcast_to(c_prompt, (s, d))], axis=0)
    mod = _ada(c_all, w_ada[0], row(b_ada[0]))
    modf = _ada(c_all, w_ada_f, row(b_ada_f))

    npt = lp // tile
    rp = npt * TILE_STEPS
    h1 = _ffn_in(x_prompt, x_sample.reshape(1, tile, d), mod,
                 row(g_ffn1[0]), w1_ffn1[0].astype(BF16), w2_ffn1[0].astype(BF16))
    u, v, gates = _inproj(h1, mod, npt, row(g_mix[0]), w_in[0].astype(BF16), row(b_in[0]))

    zs = jnp.zeros((s, N_STATE), F32)
    ys_p, re_p, im_p = _s5(u, 0, npt, zs, zs, sc, chained=True)
    ys_s, re_s, im_s = _s5(u, npt, 1, state_ssm_re[0].reshape(bs, N_STATE), state_ssm_im[0].reshape(bs, N_STATE),
                           sc, chained=False)
    h2_p = _mix(h1, ys_p, v, gates, mod, PROMPT_GROUP, 0, jnp.zeros((HIST, s, d), F32), p, chained=True)
    h2_s = _mix(h1, ys_s, v, gates, mod, SAMPLE_GROUP, npt, cache_conv[0].transpose(1, 0, 2), p, chained=False)

    ffn2 = (row(g_ffn2[0]), w1_ffn2[0].astype(BF16), w2_ffn2[0].astype(BF16), row(g_final))
    y_prompt = _ffn_out(h2_p, mod, modf, PROMPT_GROUP, *ffn2)
    y_sample = _ffn_out(h2_s, mod, modf, SAMPLE_GROUP, *ffn2).reshape(bs, ls, d)
    conv_p = v[rp - HIST:rp, s - 1, :].reshape(1, bp, HIST, d)
    conv_s = v[rp + TILE_STEPS - HIST:].transpose(1, 0, 2)[None]
    st = lambda a, n: a[:n].reshape(1, n, N_GROUPS, SSM_STATE)
    return (y_prompt, y_sample, st(re_p, bp), st(im_p, bp), conv_p, st(re_s, bs), st(im_s, bs), conv_s)
```

```python
import functools

import jax
import jax.numpy as jnp
from jax.experimental import pallas as pl
from jax.experimental.pallas import tpu as pltpu

D_MODEL = 1024
N_GROUPS = 64
SSM_GROUP = 16
SSM_STATE = 64
CONV_WIDTH = 31
D_FF = 2816
N_MOD = 9
EPS = 1e-6

SUBLANES = 8
LANES = 128
MXU_DIM = 256
VMEM_LIMIT_BYTES = 56 * 1024 * 1024

GROUPS_PER_BLOCK = MXU_DIM // SSM_GROUP
N_KBLOCKS = N_GROUPS // GROUPS_PER_BLOCK
STATE_PER_BLOCK = GROUPS_PER_BLOCK * SSM_STATE
N_STATE = N_GROUPS * SSM_STATE
HIST = CONV_WIDTH - 1
TILE_STEPS = 64
SCAN_LANE_BLOCKS = 4
ROW_SPLITS = 2
CONV_PAIRS = D_MODEL // LANES // 2
CONV_CHUNK = 8
SAMPLE_GROUP, PROMPT_GROUP = 0, 1

BF16 = jnp.bfloat16
F32 = jnp.float32


def _const_spec(shape):
    return pl.BlockSpec(shape, lambda *_: (0,) * len(shape), pipeline_mode=pl.Buffered(1))


def _params():
    return pltpu.CompilerParams(dimension_semantics=("arbitrary",), vmem_limit_bytes=VMEM_LIMIT_BYTES)


def _rms_mod(x, g, shift, scale):
    y = x * jax.lax.rsqrt(jnp.mean(x * x, axis=-1, keepdims=True) + EPS) * g
    return y * (1.0 + scale) + shift


def _tok_spec(width, tile0=0, last=None):
    if last is None:
        return pl.BlockSpec((TILE_STEPS, SUBLANES, width), lambda i: (i + tile0, 0, 0))
    return pl.BlockSpec((TILE_STEPS, SUBLANES, width), lambda i: (jnp.minimum(i + tile0, last), 0, 0))


def _mod_spec(j, n_prompt_tiles=None, group=None):
    if group is not None:
        return pl.BlockSpec((SUBLANES, D_MODEL), lambda i: (group, j))
    return pl.BlockSpec((SUBLANES, D_MODEL),
                        lambda i: (jnp.where(i < n_prompt_tiles, PROMPT_GROUP, SAMPLE_GROUP), j))


def _ada_body(c_ref, w_ref, b_ref, o_ref):
    a = jax.nn.silu(c_ref[...]).astype(BF16)
    o_ref[...] = jnp.dot(a, w_ref[...].astype(BF16), preferred_element_type=F32) + b_ref[...]


def _ada(c, w, b):
    r, n = c.shape[0], w.shape[1]
    tn = 1024
    return pl.pallas_call(
        _ada_body,
        grid=(n // tn,),
        in_specs=[pl.BlockSpec((r, D_MODEL), lambda j: (0, 0)),
                  pl.BlockSpec((D_MODEL, tn), lambda j: (0, j)),
                  pl.BlockSpec((1, tn), lambda j: (0, j))],
        out_specs=pl.BlockSpec((r, tn), lambda j: (0, j)),
        out_shape=jax.ShapeDtypeStruct((r, n), F32),
        compiler_params=_params(),
        name="ada",
    )(c, w, b)


def _rows_to_slabs(x_ref, stage_scr):
    for j in range(D_MODEL // LANES):
        for c in range(SUBLANES):
            stage_scr[j, pl.ds(c, TILE_STEPS, stride=SUBLANES), :] = (
                x_ref[0, pl.ds(c * TILE_STEPS, TILE_STEPS), pl.ds(j * LANES, LANES)])


def _staged_slabs(stage_scr):
    return jnp.concatenate([stage_scr[j].reshape(TILE_STEPS, SUBLANES, LANES)
                            for j in range(D_MODEL // LANES)], axis=-1)


def _slabs_to_rows(h, stage_scr, o_ref):
    for j in range(D_MODEL // LANES):
        stage_scr[j] = h[:, :, j * LANES:(j + 1) * LANES].reshape(TILE_STEPS * SUBLANES, LANES)
    for j in range(D_MODEL // LANES):
        for c in range(SUBLANES):
            o_ref[0, pl.ds(c * TILE_STEPS, TILE_STEPS), pl.ds(j * LANES, LANES)] = (
                stage_scr[j, pl.ds(c, TILE_STEPS, stride=SUBLANES), :])


def _swiglu_residual(x_all, sh_ref, sc_ref, gt_ref, g_ref, w1_ref, w2_ref):
    ts_all, s, d = x_all.shape
    ts = ts_all // ROW_SPLITS
    parts = []
    for part in range(ROW_SPLITS):
        x = x_all[part * ts:(part + 1) * ts]
        n = _rms_mod(x, g_ref[...], sh_ref[...], sc_ref[...])
        gu = jnp.dot(n.reshape(ts * s, d).astype(BF16), w1_ref[...], preferred_element_type=F32)
        a = (jax.nn.silu(gu[:, :D_FF]) * gu[:, D_FF:]).astype(BF16)
        y = jnp.dot(a, w2_ref[...], preferred_element_type=F32).reshape(ts, s, d)
        parts.append(x + 0.5 * gt_ref[...] * y)
    return jnp.concatenate(parts, axis=0)


def _ffn_in_body(xp_ref, xs_ref, sh_ref, sc_ref, gt_ref, g_ref, w1_ref, w2_ref, o_ref, stage_scr,
                 *, n_prompt_tiles):
    i = pl.program_id(0)

    @pl.when(i < n_prompt_tiles)
    def _():
        _rows_to_slabs(xp_ref, stage_scr)

    @pl.when(i >= n_prompt_tiles)
    def _():
        _rows_to_slabs(xs_ref, stage_scr)

    o_ref[...] = _swiglu_residual(_staged_slabs(stage_scr), sh_ref, sc_ref, gt_ref, g_ref, w1_ref, w2_ref)


def _ffn_in(x_prompt, x_sample, mod, g, w1, w2):
    d = D_MODEL
    tile = SUBLANES * TILE_STEPS
    npt = x_prompt.shape[1] // tile
    return pl.pallas_call(
        functools.partial(_ffn_in_body, n_prompt_tiles=npt),
        grid=(npt + 1,),
        in_specs=[pl.BlockSpec((1, tile, d), lambda i: (0, jnp.minimum(i, npt - 1), 0)),
                  pl.BlockSpec((1, tile, d), lambda i: (0, 0, 0)),
                  _mod_spec(0, npt), _mod_spec(1, npt), _mod_spec(2, npt),
                  _const_spec((1, d)), _const_spec(w1.shape), _const_spec(w2.shape)],
        out_specs=_tok_spec(d),
        out_shape=jax.ShapeDtypeStruct(((npt + 1) * TILE_STEPS, SUBLANES, d), F32),
        scratch_shapes=[pltpu.VMEM((d // LANES, tile, LANES), F32)],
        compiler_params=_params(),
        name="ffn_in",
    )(x_prompt, x_sample, mod, mod, mod, g, w1, w2)


def _ffn_out_body(x_ref, sh_ref, sc_ref, gt_ref, g_ref, w1_ref, w2_ref, gf_ref, shf_ref, scf_ref, o_ref, stage_scr):
    h = _swiglu_residual(x_ref[...], sh_ref, sc_ref, gt_ref, g_ref, w1_ref, w2_ref)
    _slabs_to_rows(_rms_mod(h, gf_ref[...], shf_ref[...], scf_ref[...]), stage_scr, o_ref)


def _ffn_out(x, mod, modf, group, g, w1, w2, g_final):
    d = D_MODEL
    tile = SUBLANES * TILE_STEPS
    n_tiles = x.shape[0] // TILE_STEPS
    return pl.pallas_call(
        _ffn_out_body,
        grid=(n_tiles,),
        in_specs=[_tok_spec(d), _mod_spec(6, group=group), _mod_spec(7, group=group), _mod_spec(8, group=group),
                  _const_spec((1, d)), _const_spec(w1.shape), _const_spec(w2.shape),
                  _const_spec((1, d)), _mod_spec(0, group=group), _mod_spec(1, group=group)],
        out_specs=pl.BlockSpec((1, tile, d), lambda i: (0, i, 0)),
        out_shape=jax.ShapeDtypeStruct((1, n_tiles * tile, d), F32),
        scratch_shapes=[pltpu.VMEM((d // LANES, tile, LANES), F32)],
        compiler_params=_params(),
        name="ffn_out",
    )(x, mod, mod, mod, g, w1, w2, g_final, modf, modf)


def _inproj_body(h_ref, sh_ref, sc_ref, g_ref, w_ref, b_ref, u_ref, v_ref, gates_ref):
    ts_all, s, d = h_ref.shape
    ts = ts_all // ROW_SPLITS
    for part in range(ROW_SPLITS):
        rows = pl.ds(part * ts, ts)
        n = _rms_mod(h_ref[rows], g_ref[...], sh_ref[...], sc_ref[...])
        proj = jnp.dot(n.reshape(ts * s, d).astype(BF16), w_ref[...], preferred_element_type=F32) + b_ref[...]
        u_ref[rows] = proj[:, :d].reshape(ts, s, d)
        v = proj[:, d:2 * d] * jax.nn.sigmoid(proj[:, 2 * d:3 * d])
        v_ref[rows] = v.reshape(ts, s, d)
        gates_ref[rows] = jax.nn.sigmoid(proj[:, 3 * d:]).reshape(ts, s, 2 * d)


def _inproj(h, mod, n_prompt_tiles, g, w_in, b_in):
    r, s, d = h.shape
    return pl.pallas_call(
        _inproj_body,
        grid=(r // TILE_STEPS,),
        in_specs=[_tok_spec(d), _mod_spec(3, n_prompt_tiles), _mod_spec(4, n_prompt_tiles),
                  _const_spec((1, d)), _const_spec(w_in.shape), _const_spec(b_in.shape)],
        out_specs=[_tok_spec(d), _tok_spec(d), _tok_spec(2 * d)],
        out_shape=[jax.ShapeDtypeStruct(h.shape, F32), jax.ShapeDtypeStruct(h.shape, F32),
                   jax.ShapeDtypeStruct((r, s, 2 * d), F32)],
        compiler_params=_params(),
        name="inproj",
    )(h, mod, mod, g, w_in, b_in)


_F_A1R, _F_A1I, _F_A2R, _F_A2I, _F_A4R, _F_A4I, _F_AR, _F_AI = range(8)


def _cmul(ar, ai, xr, xi):
    return ar * xr - ai * xi, ar * xi + ai * xr


def _scan_group(x_scr, ab_ref, fix_ref, cr_scr, ci_scr, kblock, group, steps, chained):
    blocks = range(group * SCAN_LANE_BLOCKS, (group + 1) * SCAN_LANE_BLOCKS)
    re_l = [pl.ds(j * LANES, LANES) for j in blocks]
    im_l = [pl.ds(STATE_PER_BLOCK + j * LANES, LANES) for j in blocks]
    c_l = [pl.ds(kblock * STATE_PER_BLOCK + j * LANES, LANES) for j in blocks]
    ar = [ab_ref[0, :, c] for c in c_l]
    ai = [ab_ref[1, :, c] for c in c_l]
    n = len(re_l)

    def rows(t):
        return pl.ds(pl.multiple_of(t * SUBLANES, SUBLANES), SUBLANES)

    def sweep(t, s):
        out = []
        for j in range(n):
            pr, pi = _cmul(ar[j], ai[j], s[2 * j], s[2 * j + 1])
            nr = pr + x_scr[rows(t), re_l[j]]
            ni = pi + x_scr[rows(t), im_l[j]]
            x_scr[rows(t), re_l[j]] = nr
            x_scr[rows(t), im_l[j]] = ni
            out += [nr, ni]
        return tuple(out)

    if not chained:
        s0 = []
        for j in range(n):
            s0 += [cr_scr[:, c_l[j]], ci_scr[:, c_l[j]]]
        s = jax.lax.fori_loop(0, steps, sweep, tuple(s0), unroll=True)
        for j in range(n):
            cr_scr[:, c_l[j]] = s[2 * j]
            ci_scr[:, c_l[j]] = s[2 * j + 1]
        return

    zero = jnp.zeros((SUBLANES, LANES), F32)
    ends = jax.lax.fori_loop(0, steps, sweep, (zero,) * (2 * n), unroll=True)
    sub = jax.lax.broadcasted_iota(jnp.int32, (SUBLANES, LANES), 0)
    starts = []
    for j in range(n):
        fx = [fix_ref[pl.ds(c * SUBLANES, SUBLANES), c_l[j]] for c in range(8)]
        er, ei = ends[2 * j], ends[2 * j + 1]
        kr = jnp.where(sub == 0, cr_scr[0:1, c_l[j]], pltpu.roll(er, 1, 0))
        ki = jnp.where(sub == 0, ci_scr[0:1, c_l[j]], pltpu.roll(ei, 1, 0))
        for d, (fr, fi) in ((1, (fx[_F_A1R], fx[_F_A1I])), (2, (fx[_F_A2R], fx[_F_A2I])),
                            (4, (fx[_F_A4R], fx[_F_A4I]))):
            pr, pi = _cmul(fr, fi, pltpu.roll(kr, d, 0), pltpu.roll(ki, d, 0))
            kr, ki = kr + pr, ki + pi
        pr, pi = _cmul(fx[_F_AR], fx[_F_AI], kr, ki)
        cr_scr[0:1, c_l[j]] = (pr + er)[SUBLANES - 1:SUBLANES, :]
        ci_scr[0:1, c_l[j]] = (pi + ei)[SUBLANES - 1:SUBLANES, :]
        starts += list(_cmul(ar[j], ai[j], kr, ki))

    def fixup(t, v):
        out = []
        for j in range(n):
            x_scr[rows(t), re_l[j]] = x_scr[rows(t), re_l[j]] + v[2 * j]
            x_scr[rows(t), im_l[j]] = x_scr[rows(t), im_l[j]] + v[2 * j + 1]
            out += list(_cmul(ar[j], ai[j], v[2 * j], v[2 * j + 1]))
        return tuple(out)

    jax.lax.fori_loop(0, steps, fixup, tuple(starts), unroll=True)


def _s5_body(u_ref, sr0_ref, si0_ref, bbd_ref, cbd_ref, ab_ref, fix_ref, dskip_ref,
             y_ref, sr_ref, si_ref, x_scr, cr_scr, ci_scr, *, chained):
    @pl.when(pl.program_id(0) == 0)
    def _():
        cr_scr[...] = sr0_ref[...]
        ci_scr[...] = si0_ref[...]

    ts, s, d = u_ref.shape
    u = u_ref[...].reshape(ts * s, d)
    ub = u.astype(BF16)
    def cols_of(k):
        return slice(k * MXU_DIM, (k + 1) * MXU_DIM)

    def bu(k):
        x_scr[k % 2] = jnp.dot(ub[:, cols_of(k)], bbd_ref[k], preferred_element_type=F32)

    bu(0)
    for k in range(N_KBLOCKS):
        cols = cols_of(k)
        xk = x_scr.at[k % 2]
        if k + 1 < N_KBLOCKS:
            bu(k + 1)
        for group in range(STATE_PER_BLOCK // LANES // SCAN_LANE_BLOCKS):
            _scan_group(xk, ab_ref, fix_ref, cr_scr, ci_scr, k, group, ts, chained)
        yk = jnp.dot(xk[...].astype(BF16), cbd_ref[k], preferred_element_type=F32)
        y_ref[:, :, cols] = (yk + dskip_ref[:, cols] * u[:, cols]).reshape(ts, s, MXU_DIM)
    sr_ref[...] = cr_scr[...]
    si_ref[...] = ci_scr[...]


def _s5(u, tile0, n_tiles, sr0, si0, sc, chained):
    s, d = SUBLANES, D_MODEL
    st = _const_spec((s, N_STATE))
    st_out = pl.BlockSpec((s, N_STATE), lambda i: (0, 0))
    return pl.pallas_call(
        functools.partial(_s5_body, chained=chained),
        grid=(n_tiles,),
        in_specs=[_tok_spec(d, tile0), st, st, _const_spec(sc['bbd'].shape), _const_spec(sc['cbd'].shape),
                  _const_spec(sc['ab'].shape), _const_spec(sc['fix'].shape), _const_spec(sc['dskip'].shape)],
        out_specs=[_tok_spec(d), st_out, st_out],
        out_shape=[jax.ShapeDtypeStruct((n_tiles * TILE_STEPS, s, d), F32),
                   jax.ShapeDtypeStruct((s, N_STATE), F32), jax.ShapeDtypeStruct((s, N_STATE), F32)],
        scratch_shapes=[pltpu.VMEM((2, TILE_STEPS * s, 2 * STATE_PER_BLOCK), F32),
                        pltpu.VMEM((s, N_STATE), F32), pltpu.VMEM((s, N_STATE), F32)],
        compiler_params=_params(),
        name="s5_chained" if chained else "s5",
    )(u, sr0, si0, sc['bbd'], sc['cbd'], sc['ab'], sc['fix'], sc['dskip'])


def _s5_constants(lam_re, lam_im, log_dt, b_re, b_im, c_re, c_im, d_skip, steps):
    dt = jnp.exp(log_dt)[:, None]
    mag = jnp.exp(lam_re * dt)
    ar = mag * jnp.cos(lam_im * dt)
    ai = mag * jnp.sin(lam_im * dt)
    den = lam_re * lam_re + lam_im * lam_im
    kr = ((ar - 1.0) * lam_re + ai * lam_im) / den
    ki = (ai * lam_re - (ar - 1.0) * lam_im) / den
    bbr = kr[..., None] * b_re - ki[..., None] * b_im
    bbi = kr[..., None] * b_im + ki[..., None] * b_re
    eye = jnp.eye(GROUPS_PER_BLOCK, dtype=F32)

    def b_blocks(bb):
        bb = bb.reshape(N_KBLOCKS, GROUPS_PER_BLOCK, SSM_STATE, SSM_GROUP)
        return jnp.einsum('kgpm,gh->kgmhp', bb, eye).reshape(N_KBLOCKS, MXU_DIM, STATE_PER_BLOCK)

    def c_blocks(cc):
        cc = cc.reshape(N_KBLOCKS, GROUPS_PER_BLOCK, SSM_GROUP, SSM_STATE)
        return jnp.einsum('kgmp,gh->kgphm', cc, eye).reshape(N_KBLOCKS, STATE_PER_BLOCK, MXU_DIM)

    bbd = jnp.concatenate([b_blocks(bbr), b_blocks(bbi)], axis=2).astype(BF16)
    cbd = jnp.concatenate([c_blocks(c_re), -c_blocks(c_im)], axis=1).astype(BF16)

    a = (ar.reshape(1, N_STATE), ai.reshape(1, N_STATE))
    ab = jnp.stack([jnp.broadcast_to(a[0], (SUBLANES, N_STATE)), jnp.broadcast_to(a[1], (SUBLANES, N_STATE))])

    def csq(x):
        return _cmul(x[0], x[1], x[0], x[1])

    assert steps & (steps - 1) == 0, "a^steps by repeated squaring"
    big = a
    for _ in range(steps.bit_length() - 1):
        big = csq(big)
    big2 = csq(big)
    big4 = csq(big2)
    sub = jnp.arange(SUBLANES, dtype=jnp.int32)[:, None]
    slabs = []
    for d, pw in ((1, big), (2, big2), (4, big4)):
        slabs += [jnp.where(sub >= d, pw[0], 0.0), jnp.where(sub >= d, pw[1], 0.0)]
    slabs += [jnp.broadcast_to(big[0], (SUBLANES, N_STATE)), jnp.broadcast_to(big[1], (SUBLANES, N_STATE))]
    fix = jnp.concatenate(slabs, axis=0)
    return dict(bbd=bbd, cbd=cbd, ab=ab, fix=fix, dskip=d_skip.reshape(1, D_MODEL))


def _depthwise_conv(ext_scr, wdw_ref, bdw_ref, conv_scr, steps):
    chunks = steps // CONV_CHUNK

    def unit(it, carry):
        j = it // chunks
        t0 = (it % chunks) * CONV_CHUNK
        w = wdw_ref[j].astype(F32)
        for t in range(CONV_CHUNK):
            x = ext_scr[j, pl.ds(t0 + t, CONV_WIDTH)].astype(F32)
            conv_scr[j, t0 + t] = bdw_ref[j] + jnp.sum(w * x, axis=0)
        return carry

    jax.lax.fori_loop(0, CONV_PAIRS * chunks, unit, 0)


def _mix_body(h_ref, ys_ref, v_ref, hist0_ref, gates_ref, gt_ref,
              wglu_ref, wdw_ref, bdw_ref, lng_ref, lnb_ref, wpw_ref, bpw_ref, wout_ref,
              o_ref, ext_scr, tail_scr, conv_scr, *, chained):
    ts, s, d = h_ref.shape

    @pl.when(pl.program_id(0) == 0)
    def _():
        tail_scr[...] = hist0_ref[...]

    cur_tail = v_ref[pl.ds(ts - HIST, HIST), :, :]
    if chained:
        sub = jax.lax.broadcasted_iota(jnp.int32, (HIST, s, d), 1)
        hist = jnp.where(sub == 0, pltpu.roll(tail_scr[...], 1, 1), pltpu.roll(cur_tail, 1, 1))
    else:
        hist = tail_scr[...]
    def paired(a, j):
        lo = a[:, :, j * LANES:(j + 1) * LANES]
        hi = a[:, :, (j + CONV_PAIRS) * LANES:(j + CONV_PAIRS + 1) * LANES]
        return jnp.concatenate([lo, hi], axis=1).astype(BF16)

    for j in range(CONV_PAIRS):
        ext_scr[j, pl.ds(0, HIST)] = paired(hist, j)
        ext_scr[j, pl.ds(HIST, ts)] = paired(v_ref[...], j)
    tail_scr[...] = cur_tail
    _depthwise_conv(ext_scr, wdw_ref, bdw_ref, conv_scr, ts)

    tp = ts // ROW_SPLITS
    mp = tp * s
    for part in range(ROW_SPLITS):
        rows = pl.ds(part * tp, tp)
        conv = jnp.concatenate(
            [conv_scr[j % CONV_PAIRS, rows][:, (j // CONV_PAIRS) * s:(j // CONV_PAIRS + 1) * s, :]
             for j in range(d // LANES)], axis=-1)
        mu = jnp.mean(conv, axis=-1, keepdims=True)
        cen = conv - mu
        var = jnp.mean(cen * cen, axis=-1, keepdims=True)
        ln = cen * jax.lax.rsqrt(var + EPS) * lng_ref[...] + lnb_ref[...]
        vb = jax.nn.silu(ln).reshape(mp, d).astype(BF16)
        y_b = jnp.dot(vb, wpw_ref[...], preferred_element_type=F32) + bpw_ref[...]
        ga = jax.nn.gelu(ys_ref[rows].reshape(mp, d)).astype(BF16)
        ag = jnp.dot(ga, wglu_ref[...], preferred_element_type=F32)
        y_a = ag[:, :d] * jax.nn.sigmoid(ag[:, d:])
        gates = gates_ref[rows].reshape(mp, 2 * d)
        merged = (gates[:, :d] * y_a + gates[:, d:] * y_b).astype(BF16)
        out = jnp.dot(merged, wout_ref[...], preferred_element_type=F32).reshape(tp, s, d)
        o_ref[rows] = h_ref[rows] + gt_ref[...] * out


def _mix(h, ys, v, gates, mod, group, tile0, hist0, p, chained):
    s, d = SUBLANES, D_MODEL
    n_tiles = ys.shape[0] // TILE_STEPS
    return pl.pallas_call(
        functools.partial(_mix_body, chained=chained),
        grid=(n_tiles,),
        in_specs=[_tok_spec(d, tile0), _tok_spec(d), _tok_spec(d, tile0), _const_spec(hist0.shape),
                  _tok_spec(2 * d, tile0), _mod_spec(5, group=group),
                  _const_spec(p['w_glu'].shape), _const_spec(p['w_dw'].shape), _const_spec(p['b_dw'].shape),
                  _const_spec((1, d)), _const_spec((1, d)), _const_spec(p['w_pw'].shape),
                  _const_spec((1, d)), _const_spec(p['w_out'].shape)],
        out_specs=_tok_spec(d),
        out_shape=jax.ShapeDtypeStruct(ys.shape, F32),
        scratch_shapes=[pltpu.VMEM((CONV_PAIRS, TILE_STEPS + HIST, 2 * s, LANES), BF16),
                        pltpu.VMEM((HIST, s, d), F32),
                        pltpu.VMEM((CONV_PAIRS, TILE_STEPS, 2 * s, LANES), F32)],
        compiler_params=_params(),
        name="mix_chained" if chained else "mix",
    )(h, ys, v, hist0, gates, mod, p['w_glu'], p['w_dw'], p['b_dw'], p['ln_g'], p['ln_b'],
      p['w_pw'], p['b_pw'], p['w_out'])


def kernel(x_prompt, x_sample, c_prompt, c_sample, state_ssm_re, state_ssm_im, cache_conv, w_ada, b_ada, g_ffn1, w1_ffn1, w2_ffn1, g_mix, w_in, b_in, lam_re, lam_im, log_dt, b_re, b_im, c_re, c_im, d_skip, w_glu, w_dw, b_dw, ln_g, ln_b, w_pw, b_pw, w_out, g_ffn2, w1_ffn2, w2_ffn2, g_final, w_ada_f, b_ada_f):
    d, s = D_MODEL, SUBLANES
    bp, lp = x_prompt.shape[:2]
    bs, ls = x_sample.shape[:2]
    assert w_ada.shape[0] == 1 and bp == 1 and bs == s, "single layer, one prompt, 8 sample sequences"
    tile = s * TILE_STEPS
    assert lp % tile == 0 and ls == TILE_STEPS and TILE_STEPS >= HIST
    row = lambda a: a.reshape(1, -1)
    nlb = d // LANES
    w_dw_blocks = jnp.broadcast_to(w_dw[0].reshape(CONV_WIDTH, nlb, 1, LANES).transpose(1, 0, 2, 3),
                                   (nlb, CONV_WIDTH, s, LANES))
    b_dw_blocks = jnp.broadcast_to(b_dw[0].reshape(nlb, 1, LANES), (nlb, s, LANES))
    p = dict(
        w_glu=w_glu[0].astype(BF16),
        w_dw=jnp.concatenate([w_dw_blocks[:CONV_PAIRS], w_dw_blocks[CONV_PAIRS:]], axis=2).astype(BF16),
        b_dw=jnp.concatenate([b_dw_blocks[:CONV_PAIRS], b_dw_blocks[CONV_PAIRS:]], axis=1),
        ln_g=row(ln_g[0]), ln_b=row(ln_b[0]),
        w_pw=w_pw[0].astype(BF16), b_pw=row(b_pw[0]), w_out=w_out[0].astype(BF16),
    )
    sc = _s5_constants(lam_re[0], lam_im[0], log_dt[0], b_re[0], b_im[0], c_re[0], c_im[0], d_skip[0], TILE_STEPS)

    c_all = jnp.concatenate([c_sample, jnp.broadcast_to(c_prompt, (s, d))], axis=0)
    mod = _ada(c_all, w_ada[0], row(b_ada[0]))
    modf = _ada(c_all, w_ada_f, row(b_ada_f))

    npt = lp // tile
    rp = npt * TILE_STEPS
    h1 = _ffn_in(x_prompt, x_sample.reshape(1, tile, d), mod,
                 row(g_ffn1[0]), w1_ffn1[0].astype(BF16), w2_ffn1[0].astype(BF16))
    u, v, gates = _inproj(h1, mod, npt, row(g_mix[0]), w_in[0].astype(BF16), row(b_in[0]))

    zs = jnp.zeros((s, N_STATE), F32)
    ys_p, re_p, im_p = _s5(u, 0, npt, zs, zs, sc, chained=True)
    ys_s, re_s, im_s = _s5(u, npt, 1, state_ssm_re[0].reshape(bs, N_STATE), state_ssm_im[0].reshape(bs, N_STATE),
                           sc, chained=False)
    h2_p = _mix(h1, ys_p, v, gates, mod, PROMPT_GROUP, 0, jnp.zeros((HIST, s, d), F32), p, chained=True)
    h2_s = _mix(h1, ys_s, v, gates, mod, SAMPLE_GROUP, npt, cache_conv[0].transpose(1, 0, 2), p, chained=False)

    ffn2 = (row(g_ffn2[0]), w1_ffn2[0].astype(BF16), w2_ffn2[0].astype(BF16), row(g_final))
    y_prompt = _ffn_out(h2_p, mod, modf, PROMPT_GROUP, *ffn2)
    y_sample = _ffn_out(h2_s, mod, modf, SAMPLE_GROUP, *ffn2).reshape(bs, ls, d)
    conv_p = v[rp - HIST:rp, s - 1, :].reshape(1, bp, HIST, d)
    conv_s = v[rp + TILE_STEPS - HIST:].transpose(1, 0, 2)[None]
    st = lambda a, n: a[:n].reshape(1, n, N_GROUPS, SSM_STATE)
    return (y_prompt, y_sample, st(re_p, bp), st(im_p, bp), conv_p, st(re_s, bs), st(im_s, bs), conv_s)
```

```python
import functools

import jax
import jax.numpy as jnp
import numpy as np
from jax.experimental import pallas as pl
from jax.experimental.pallas import tpu as pltpu

D_MODEL = 1024
N_GROUPS = 64
SSM_GROUP = 16
SSM_STATE = 64
CONV_WIDTH = 31
D_FF = 2816
N_MOD = 9
EPS = 1e-6

SUBLANES = 8
LANES = 128
MXU_DIM = 256
VMEM_LIMIT_BYTES = 56 * 1024 * 1024

GROUPS_PER_BLOCK = MXU_DIM // SSM_GROUP
N_KBLOCKS = N_GROUPS // GROUPS_PER_BLOCK
STATE_PER_BLOCK = GROUPS_PER_BLOCK * SSM_STATE
N_STATE = N_GROUPS * SSM_STATE
HIST = CONV_WIDTH - 1
TILE_STEPS = 64
SCAN_LANE_BLOCKS = 4
ROW_SPLITS = 2
CONV_PAIRS = D_MODEL // LANES // 2
CONV_CHUNK = 8
SAMPLE_GROUP, PROMPT_GROUP = 0, 1
ADA_COLS = 1024

BF16 = jnp.bfloat16
F32 = jnp.float32


def _const_spec(shape):
    return pl.BlockSpec(shape, lambda *_: (0,) * len(shape), pipeline_mode=pl.Buffered(1))


def _params():
    return pltpu.CompilerParams(dimension_semantics=("arbitrary",), vmem_limit_bytes=VMEM_LIMIT_BYTES)


def _rms_mod(x, g, shift, scale):
    y = x * jax.lax.rsqrt(jnp.mean(x * x, axis=-1, keepdims=True) + EPS) * g
    return y * (1.0 + scale) + shift


def _tok_spec(width, tile0=0, last=None):
    if last is None:
        return pl.BlockSpec((TILE_STEPS, SUBLANES, width), lambda i: (i + tile0, 0, 0))
    return pl.BlockSpec((TILE_STEPS, SUBLANES, width), lambda i: (jnp.minimum(i + tile0, last), 0, 0))


def _mod_spec(j, n_prompt_tiles=None, group=None):
    if group is not None:
        return pl.BlockSpec((SUBLANES, D_MODEL), lambda i: (group, j))
    return pl.BlockSpec((SUBLANES, D_MODEL),
                        lambda i: (jnp.where(i < n_prompt_tiles, PROMPT_GROUP, SAMPLE_GROUP), j))


def _ada_body(c_ref, w_ref, b_ref, o_ref):
    a = jax.nn.silu(c_ref[...]).astype(BF16)
    o_ref[...] = jnp.dot(a, w_ref[...].astype(BF16), preferred_element_type=F32) + b_ref[...]


def _ada(c, w, b):
    r, n = c.shape[0], w.shape[1]
    tn = ADA_COLS
    return pl.pallas_call(
        _ada_body,
        grid=(n // tn,),
        in_specs=[pl.BlockSpec((r, D_MODEL), lambda j: (0, 0)),
                  pl.BlockSpec((D_MODEL, tn), lambda j: (0, j)),
                  pl.BlockSpec((1, tn), lambda j: (0, j))],
        out_specs=pl.BlockSpec((r, tn), lambda j: (0, j)),
        out_shape=jax.ShapeDtypeStruct((r, n), F32),
        compiler_params=_params(),
        name="ada",
    )(c, w, b)


def _rows_to_slabs(x, stage_scr):
    for j in range(D_MODEL // LANES):
        for c in range(SUBLANES):
            stage_scr[j, pl.ds(c, TILE_STEPS, stride=SUBLANES), :] = (
                x[c * TILE_STEPS:(c + 1) * TILE_STEPS, j * LANES:(j + 1) * LANES])


def _staged_slabs(stage_scr):
    return jnp.concatenate([stage_scr[j].reshape(TILE_STEPS, SUBLANES, LANES)
                            for j in range(D_MODEL // LANES)], axis=-1)


def _slabs_to_rows(h, stage_scr, o_ref):
    for j in range(D_MODEL // LANES):
        stage_scr[j] = h[:, :, j * LANES:(j + 1) * LANES].reshape(TILE_STEPS * SUBLANES, LANES)
    for j in range(D_MODEL // LANES):
        for c in range(SUBLANES):
            o_ref[0, pl.ds(c * TILE_STEPS, TILE_STEPS), pl.ds(j * LANES, LANES)] = (
                stage_scr[j, pl.ds(c, TILE_STEPS, stride=SUBLANES), :])


def _swiglu_residual(x_all, sh_ref, sc_ref, gt_ref, g_ref, w1_ref, w2_ref):
    ts_all, s, d = x_all.shape
    ts = ts_all // ROW_SPLITS
    parts = []
    for part in range(ROW_SPLITS):
        x = x_all[part * ts:(part + 1) * ts]
        n = _rms_mod(x, g_ref[...], sh_ref[...], sc_ref[...])
        gu = jnp.dot(n.reshape(ts * s, d).astype(BF16), w1_ref[...], preferred_element_type=F32)
        a = (jax.nn.silu(gu[:, :D_FF]) * gu[:, D_FF:]).astype(BF16)
        y = jnp.dot(a, w2_ref[...], preferred_element_type=F32).reshape(ts, s, d)
        parts.append(x + 0.5 * gt_ref[...] * y)
    return jnp.concatenate(parts, axis=0)


def _ffn_in_body(xp_ref, xs_ref, sh_ref, sc_ref, gt_ref, g_ref, w1_ref, w2_ref, o_ref, stage_scr,
                 *, n_prompt_tiles):
    _rows_to_slabs(jnp.where(pl.program_id(0) < n_prompt_tiles, xp_ref[0], xs_ref[0]), stage_scr)
    o_ref[...] = _swiglu_residual(_staged_slabs(stage_scr), sh_ref, sc_ref, gt_ref, g_ref, w1_ref, w2_ref)


def _ffn_in(x_prompt, x_sample, mod, g, w1, w2):
    d = D_MODEL
    tile = SUBLANES * TILE_STEPS
    npt = x_prompt.shape[1] // tile
    return pl.pallas_call(
        functools.partial(_ffn_in_body, n_prompt_tiles=npt),
        grid=(npt + 1,),
        in_specs=[pl.BlockSpec((1, tile, d), lambda i: (0, jnp.minimum(i, npt - 1), 0)),
                  pl.BlockSpec((1, tile, d), lambda i: (0, 0, 0)),
                  _mod_spec(0, npt), _mod_spec(1, npt), _mod_spec(2, npt),
                  _const_spec((1, d)), _const_spec(w1.shape), _const_spec(w2.shape)],
        out_specs=_tok_spec(d),
        out_shape=jax.ShapeDtypeStruct(((npt + 1) * TILE_STEPS, SUBLANES, d), F32),
        scratch_shapes=[pltpu.VMEM((d // LANES, tile, LANES), F32)],
        compiler_params=_params(),
        name="ffn_in",
    )(x_prompt, x_sample, mod, mod, mod, g, w1, w2)


def _ffn_out_body(x_ref, sh_ref, sc_ref, gt_ref, g_ref, w1_ref, w2_ref, gf_ref, shf_ref, scf_ref, o_ref, stage_scr):
    h = _swiglu_residual(x_ref[...], sh_ref, sc_ref, gt_ref, g_ref, w1_ref, w2_ref)
    _slabs_to_rows(_rms_mod(h, gf_ref[...], shf_ref[...], scf_ref[...]), stage_scr, o_ref)


def _ffn_out(x, mod, modf, group, g, w1, w2, g_final):
    d = D_MODEL
    tile = SUBLANES * TILE_STEPS
    n_tiles = x.shape[0] // TILE_STEPS
    return pl.pallas_call(
        _ffn_out_body,
        grid=(n_tiles,),
        in_specs=[_tok_spec(d), _mod_spec(6, group=group), _mod_spec(7, group=group), _mod_spec(8, group=group),
                  _const_spec((1, d)), _const_spec(w1.shape), _const_spec(w2.shape),
                  _const_spec((1, d)), _mod_spec(0, group=group), _mod_spec(1, group=group)],
        out_specs=pl.BlockSpec((1, tile, d), lambda i: (0, i, 0)),
        out_shape=jax.ShapeDtypeStruct((1, n_tiles * tile, d), F32),
        scratch_shapes=[pltpu.VMEM((d // LANES, tile, LANES), F32)],
        compiler_params=_params(),
        name="ffn_out",
    )(x, mod, mod, mod, g, w1, w2, g_final, modf, modf)


_F_A1R, _F_A1I, _F_A2R, _F_A2I, _F_A4R, _F_A4I, _F_AR, _F_AI = range(8)
_N_FIX = 8


def _cmul(ar, ai, xr, xi):
    return ar * xr - ai * xi, ar * xi + ai * xr


def _scan_group(x_scr, ab_ref, fix_ref, cr_scr, ci_scr, kblock, group, steps, chained):
    blocks = range(group * SCAN_LANE_BLOCKS, (group + 1) * SCAN_LANE_BLOCKS)
    re_l = [pl.ds(j * LANES, LANES) for j in blocks]
    im_l = [pl.ds(STATE_PER_BLOCK + j * LANES, LANES) for j in blocks]
    c_l = [pl.ds(kblock * STATE_PER_BLOCK + j * LANES, LANES) for j in blocks]
    ar = [ab_ref[0, :, c] for c in c_l]
    ai = [ab_ref[1, :, c] for c in c_l]
    n = len(re_l)

    def rows(t):
        return pl.ds(pl.multiple_of(t * SUBLANES, SUBLANES), SUBLANES)

    def sweep(t, s):
        out = []
        for j in range(n):
            pr, pi = _cmul(ar[j], ai[j], s[2 * j], s[2 * j + 1])
            nr = pr + x_scr[rows(t), re_l[j]]
            ni = pi + x_scr[rows(t), im_l[j]]
            x_scr[rows(t), re_l[j]] = nr
            x_scr[rows(t), im_l[j]] = ni
            out += [nr, ni]
        return tuple(out)

    if not chained:
        s0 = []
        for j in range(n):
            s0 += [cr_scr[:, c_l[j]], ci_scr[:, c_l[j]]]
        s = jax.lax.fori_loop(0, steps, sweep, tuple(s0), unroll=True)
        for j in range(n):
            cr_scr[:, c_l[j]] = s[2 * j]
            ci_scr[:, c_l[j]] = s[2 * j + 1]
        return

    zero = jnp.zeros((SUBLANES, LANES), F32)
    ends = jax.lax.fori_loop(0, steps, sweep, (zero,) * (2 * n), unroll=True)
    sub = jax.lax.broadcasted_iota(jnp.int32, (SUBLANES, LANES), 0)
    starts = []
    for j in range(n):
        fx = [fix_ref[pl.ds(c * SUBLANES, SUBLANES), c_l[j]] for c in range(_N_FIX)]
        er, ei = ends[2 * j], ends[2 * j + 1]
        kr = jnp.where(sub == 0, cr_scr[0:1, c_l[j]], pltpu.roll(er, 1, 0))
        ki = jnp.where(sub == 0, ci_scr[0:1, c_l[j]], pltpu.roll(ei, 1, 0))
        for d, (fr, fi) in ((1, (fx[_F_A1R], fx[_F_A1I])), (2, (fx[_F_A2R], fx[_F_A2I])),
                            (4, (fx[_F_A4R], fx[_F_A4I]))):
            pr, pi = _cmul(fr, fi, pltpu.roll(kr, d, 0), pltpu.roll(ki, d, 0))
            kr, ki = kr + pr, ki + pi
        pr, pi = _cmul(fx[_F_AR], fx[_F_AI], kr, ki)
        cr_scr[0:1, c_l[j]] = (pr + er)[SUBLANES - 1:SUBLANES, :]
        ci_scr[0:1, c_l[j]] = (pi + ei)[SUBLANES - 1:SUBLANES, :]
        starts += list(_cmul(ar[j], ai[j], kr, ki))

    def fixup(t, v):
        out = []
        for j in range(n):
            x_scr[rows(t), re_l[j]] = x_scr[rows(t), re_l[j]] + v[2 * j]
            x_scr[rows(t), im_l[j]] = x_scr[rows(t), im_l[j]] + v[2 * j + 1]
            out += list(_cmul(ar[j], ai[j], v[2 * j], v[2 * j + 1]))
        return tuple(out)

    jax.lax.fori_loop(0, steps, fixup, tuple(starts), unroll=True)


def _proj_s5_body(h_ref, sh_ref, sc_ref, g_ref, w_ref, b_ref, sr0_ref, si0_ref,
                  bbd_ref, cbd_ref, ab_ref, fix_ref, dskip_ref,
                  y_ref, v_ref, gates_ref, sr_ref, si_ref, x_scr, cr_scr, ci_scr, *, chained):
    @pl.when(pl.program_id(0) == 0)
    def _():
        cr_scr[...] = sr0_ref[...]
        ci_scr[...] = si0_ref[...]

    ts, s, d = h_ref.shape
    m = ts * s
    nb = _rms_mod(h_ref[...], g_ref[...], sh_ref[...], sc_ref[...]).reshape(m, d).astype(BF16)

    def proj(c0, c1):
        return jnp.dot(nb, w_ref[:, c0:c1], preferred_element_type=F32) + b_ref[:, c0:c1]

    u = proj(0, d)
    ub = u.astype(BF16)

    def cols_of(k):
        return slice(k * MXU_DIM, (k + 1) * MXU_DIM)

    def bu(k):
        x_scr[k % 2] = jnp.dot(ub[:, cols_of(k)], bbd_ref[k], preferred_element_type=F32)

    bu(0)
    cv_val = None
    for k in range(N_KBLOCKS):
        cols = cols_of(k)
        xk = x_scr.at[k % 2]
        if k + 1 < N_KBLOCKS:
            bu(k + 1)
        pk = proj((k + 1) * d, (k + 2) * d)
        if k == 0:
            cv_val = pk
        elif k == 1:
            v_ref[...] = (cv_val * jax.nn.sigmoid(pk)).reshape(ts, s, d)
        else:
            gates_ref[:, :, (k - 2) * d:(k - 1) * d] = jax.nn.sigmoid(pk).reshape(ts, s, d)
        for group in range(STATE_PER_BLOCK // LANES // SCAN_LANE_BLOCKS):
            _scan_group(xk, ab_ref, fix_ref, cr_scr, ci_scr, k, group, ts, chained)
        yk = jnp.dot(xk[...].astype(BF16), cbd_ref[k], preferred_element_type=F32)
        y_ref[:, :, cols] = (yk + dskip_ref[:, cols] * u[:, cols]).reshape(ts, s, MXU_DIM)
    sr_ref[...] = cr_scr[...]
    si_ref[...] = ci_scr[...]


def _proj_s5(h, tile0, n_tiles, mod, group, g, w_in, b_in, sr0, si0, sc, chained):
    s, d = SUBLANES, D_MODEL
    assert N_KBLOCKS == 4 and w_in.shape[1] == 5 * d, "one projection quarter per S5 block"
    st = _const_spec((s, N_STATE))
    st_out = pl.BlockSpec((s, N_STATE), lambda i: (0, 0))
    rows = n_tiles * TILE_STEPS
    return pl.pallas_call(
        functools.partial(_proj_s5_body, chained=chained),
        grid=(n_tiles,),
        in_specs=[_tok_spec(d, tile0), _mod_spec(3, group=group), _mod_spec(4, group=group),
                  _const_spec((1, d)), _const_spec(w_in.shape), _const_spec(b_in.shape), st, st,
                  _const_spec(sc['bbd'].shape), _const_spec(sc['cbd'].shape),
                  _const_spec(sc['ab'].shape), _const_spec(sc['fix'].shape), _const_spec(sc['dskip'].shape)],
        out_specs=[_tok_spec(d), _tok_spec(d), _tok_spec(2 * d), st_out, st_out],
        out_shape=[jax.ShapeDtypeStruct((rows, s, d), F32), jax.ShapeDtypeStruct((rows, s, d), F32),
                   jax.ShapeDtypeStruct((rows, s, 2 * d), F32),
                   jax.ShapeDtypeStruct((s, N_STATE), F32), jax.ShapeDtypeStruct((s, N_STATE), F32)],
        scratch_shapes=[pltpu.VMEM((2, TILE_STEPS * s, 2 * STATE_PER_BLOCK), F32),
                        pltpu.VMEM((s, N_STATE), F32), pltpu.VMEM((s, N_STATE), F32)],
        compiler_params=_params(),
        name="proj_s5_chained" if chained else "proj_s5",
    )(h, mod, mod, g, w_in, b_in, sr0, si0, sc['bbd'], sc['cbd'], sc['ab'], sc['fix'], sc['dskip'])


def _s5_constants(lam_re, lam_im, log_dt, b_re, b_im, c_re, c_im, d_skip, steps):
    dt = jnp.exp(log_dt)[:, None]
    mag = jnp.exp(lam_re * dt)
    ar = mag * jnp.cos(lam_im * dt)
    ai = mag * jnp.sin(lam_im * dt)
    den = lam_re * lam_re + lam_im * lam_im
    kr = ((ar - 1.0) * lam_re + ai * lam_im) / den
    ki = (ai * lam_re - (ar - 1.0) * lam_im) / den
    bbr = kr[..., None] * b_re - ki[..., None] * b_im
    bbi = kr[..., None] * b_im + ki[..., None] * b_re
    g_of_in = np.arange(MXU_DIM) // SSM_GROUP
    g_of_state = np.arange(STATE_PER_BLOCK) // SSM_STATE
    b_mask = jnp.asarray(g_of_in[:, None] == g_of_state[None, :], F32)
    c_mask = jnp.asarray(g_of_state[:, None] == g_of_in[None, :], F32)

    def b_blocks(bb):
        rows = bb.transpose(0, 2, 1).reshape(N_KBLOCKS, MXU_DIM, SSM_STATE)
        return jnp.tile(rows, (1, 1, GROUPS_PER_BLOCK)) * b_mask

    def c_blocks(cc):
        rows = cc.transpose(0, 2, 1).reshape(N_KBLOCKS, STATE_PER_BLOCK, SSM_GROUP)
        return jnp.tile(rows, (1, 1, GROUPS_PER_BLOCK)) * c_mask

    bbd = jnp.concatenate([b_blocks(bbr), b_blocks(bbi)], axis=2).astype(BF16)
    cbd = jnp.concatenate([c_blocks(c_re), -c_blocks(c_im)], axis=1).astype(BF16)

    a = (ar.reshape(1, N_STATE), ai.reshape(1, N_STATE))
    ab = jnp.stack([jnp.broadcast_to(a[0], (SUBLANES, N_STATE)), jnp.broadcast_to(a[1], (SUBLANES, N_STATE))])

    def csq(x):
        return _cmul(x[0], x[1], x[0], x[1])

    assert steps & (steps - 1) == 0, "a^steps by repeated squaring"
    big = a
    for _ in range(steps.bit_length() - 1):
        big = csq(big)
    big2 = csq(big)
    big4 = csq(big2)
    sub = jnp.arange(SUBLANES, dtype=jnp.int32)[:, None]
    slabs = []
    for d, pw in ((1, big), (2, big2), (4, big4)):
        slabs += [jnp.where(sub >= d, pw[0], 0.0), jnp.where(sub >= d, pw[1], 0.0)]
    slabs += [jnp.broadcast_to(big[0], (SUBLANES, N_STATE)), jnp.broadcast_to(big[1], (SUBLANES, N_STATE))]
    fix = jnp.concatenate(slabs, axis=0)
    return dict(bbd=bbd, cbd=cbd, ab=ab, fix=fix, dskip=d_skip.reshape(1, D_MODEL))


def _depthwise_conv(ext_scr, wdw_ref, bdw_ref, conv_scr, steps):
    chunks = steps // CONV_CHUNK

    def unit(it, carry):
        j = it // chunks
        t0 = (it % chunks) * CONV_CHUNK
        w = wdw_ref[j].astype(F32)
        for t in range(CONV_CHUNK):
            x = ext_scr[j, pl.ds(t0 + t, CONV_WIDTH)].astype(F32)
            conv_scr[j, t0 + t] = bdw_ref[j] + jnp.sum(w * x, axis=0)
        return carry

    jax.lax.fori_loop(0, CONV_PAIRS * chunks, unit, 0)


def _mix_body(h_ref, ys_ref, v_ref, hist0_ref, gates_ref, gt_ref,
              wglu_ref, wdw_ref, bdw_ref, lng_ref, lnb_ref, wpw_ref, bpw_ref, wout_ref,
              o_ref, ext_scr, tail_scr, conv_scr, *, chained):
    ts, s, d = h_ref.shape

    @pl.when(pl.program_id(0) == 0)
    def _():
        tail_scr[...] = hist0_ref[...]

    cur_tail = v_ref[pl.ds(ts - HIST, HIST), :, :]
    if chained:
        sub = jax.lax.broadcasted_iota(jnp.int32, (HIST, s, d), 1)
        hist = jnp.where(sub == 0, pltpu.roll(tail_scr[...], 1, 1), pltpu.roll(cur_tail, 1, 1))
    else:
        hist = tail_scr[...]
    def paired(a, j):
        lo = a[:, :, j * LANES:(j + 1) * LANES]
        hi = a[:, :, (j + CONV_PAIRS) * LANES:(j + CONV_PAIRS + 1) * LANES]
        return jnp.concatenate([lo, hi], axis=1).astype(BF16)

    for j in range(CONV_PAIRS):
        ext_scr[j, pl.ds(0, HIST)] = paired(hist, j)
        ext_scr[j, pl.ds(HIST, ts)] = paired(v_ref[...], j)
    tail_scr[...] = cur_tail
    _depthwise_conv(ext_scr, wdw_ref, bdw_ref, conv_scr, ts)

    tp = ts // ROW_SPLITS
    mp = tp * s
    for part in range(ROW_SPLITS):
        rows = pl.ds(part * tp, tp)
        conv = jnp.concatenate(
            [conv_scr[j % CONV_PAIRS, rows][:, (j // CONV_PAIRS) * s:(j // CONV_PAIRS + 1) * s, :]
             for j in range(d // LANES)], axis=-1)
        mu = jnp.mean(conv, axis=-1, keepdims=True)
        cen = conv - mu
        var = jnp.mean(cen * cen, axis=-1, keepdims=True)
        ln = cen * jax.lax.rsqrt(var + EPS) * lng_ref[...] + lnb_ref[...]
        vb = jax.nn.silu(ln).reshape(mp, d).astype(BF16)
        y_b = jnp.dot(vb, wpw_ref[...], preferred_element_type=F32) + bpw_ref[...]
        ga = jax.nn.gelu(ys_ref[rows].reshape(mp, d)).astype(BF16)
        ag = jnp.dot(ga, wglu_ref[...], preferred_element_type=F32)
        y_a = ag[:, :d] * jax.nn.sigmoid(ag[:, d:])
        gates = gates_ref[rows].reshape(mp, 2 * d)
        merged = (gates[:, :d] * y_a + gates[:, d:] * y_b).astype(BF16)
        out = jnp.dot(merged, wout_ref[...], preferred_element_type=F32).reshape(tp, s, d)
        o_ref[rows] = h_ref[rows] + gt_ref[...] * out


def _mix(h, ys, v, gates, mod, group, tile0, hist0, p, chained):
    s, d = SUBLANES, D_MODEL
    n_tiles = ys.shape[0] // TILE_STEPS
    return pl.pallas_call(
        functools.partial(_mix_body, chained=chained),
        grid=(n_tiles,),
        in_specs=[_tok_spec(d, tile0), _tok_spec(d), _tok_spec(d), _const_spec(hist0.shape),
                  _tok_spec(2 * d), _mod_spec(5, group=group),
                  _const_spec(p['w_glu'].shape), _const_spec(p['w_dw'].shape), _const_spec(p['b_dw'].shape),
                  _const_spec((1, d)), _const_spec((1, d)), _const_spec(p['w_pw'].shape),
                  _const_spec((1, d)), _const_spec(p['w_out'].shape)],
        out_specs=_tok_spec(d),
        out_shape=jax.ShapeDtypeStruct(ys.shape, F32),
        scratch_shapes=[pltpu.VMEM((CONV_PAIRS, TILE_STEPS + HIST, 2 * s, LANES), BF16),
                        pltpu.VMEM((HIST, s, d), F32),
                        pltpu.VMEM((CONV_PAIRS, TILE_STEPS, 2 * s, LANES), F32)],
        compiler_params=_params(),
        name="mix_chained" if chained else "mix",
    )(h, ys, v, hist0, gates, mod, p['w_glu'], p['w_dw'], p['b_dw'], p['ln_g'], p['ln_b'],
      p['w_pw'], p['b_pw'], p['w_out'])


def kernel(x_prompt, x_sample, c_prompt, c_sample, state_ssm_re, state_ssm_im, cache_conv, w_ada, b_ada, g_ffn1, w1_ffn1, w2_ffn1, g_mix, w_in, b_in, lam_re, lam_im, log_dt, b_re, b_im, c_re, c_im, d_skip, w_glu, w_dw, b_dw, ln_g, ln_b, w_pw, b_pw, w_out, g_ffn2, w1_ffn2, w2_ffn2, g_final, w_ada_f, b_ada_f):
    d, s = D_MODEL, SUBLANES
    bp, lp = x_prompt.shape[:2]
    bs, ls = x_sample.shape[:2]
    assert w_ada.shape[0] == 1 and bp == 1 and bs == s, "single layer, one prompt, 8 sample sequences"
    tile = s * TILE_STEPS
    assert lp % tile == 0 and ls == TILE_STEPS and TILE_STEPS >= HIST
    row = lambda a: a.reshape(1, -1)
    nlb = d // LANES
    w_dw_blocks = jnp.broadcast_to(w_dw[0].reshape(CONV_WIDTH, nlb, 1, LANES).transpose(1, 0, 2, 3),
                                   (nlb, CONV_WIDTH, s, LANES))
    b_dw_blocks = jnp.broadcast_to(b_dw[0].reshape(nlb, 1, LANES), (nlb, s, LANES))
    p = dict(
        w_glu=w_glu[0].astype(BF16),
        w_dw=jnp.concatenate([w_dw_blocks[:CONV_PAIRS], w_dw_blocks[CONV_PAIRS:]], axis=2).astype(BF16),
        b_dw=jnp.concatenate([b_dw_blocks[:CONV_PAIRS], b_dw_blocks[CONV_PAIRS:]], axis=1),
        ln_g=row(ln_g[0]), ln_b=row(ln_b[0]),
        w_pw=w_pw[0].astype(BF16), b_pw=row(b_pw[0]), w_out=w_out[0].astype(BF16),
    )
    sc = _s5_constants(lam_re[0], lam_im[0], log_dt[0], b_re[0], b_im[0], c_re[0], c_im[0], d_skip[0], TILE_STEPS)

    c_all = jnp.concatenate([c_sample, jnp.broadcast_to(c_prompt, (s, d))], axis=0)
    mod = _ada(c_all, w_ada[0], row(b_ada[0]))
    modf = _ada(c_all, w_ada_f, row(b_ada_f))

    npt = lp // tile
    rp = npt * TILE_STEPS
    h1 = _ffn_in(x_prompt, x_sample.reshape(1, tile, d), mod,
                 row(g_ffn1[0]), w1_ffn1[0].astype(BF16), w2_ffn1[0].astype(BF16))
    inproj = (row(g_mix[0]), w_in[0].astype(BF16), row(b_in[0]))
    zs = jnp.zeros((s, N_STATE), F32)
    ys_p, v_p, gates_p, re_p, im_p = _proj_s5(h1, 0, npt, mod, PROMPT_GROUP, *inproj, zs, zs, sc, chained=True)
    ys_s, v_s, gates_s, re_s, im_s = _proj_s5(
        h1, npt, 1, mod, SAMPLE_GROUP, *inproj,
        state_ssm_re[0].reshape(bs, N_STATE), state_ssm_im[0].reshape(bs, N_STATE), sc, chained=False)
    h2_p = _mix(h1, ys_p, v_p, gates_p, mod, PROMPT_GROUP, 0, jnp.zeros((HIST, s, d), F32), p, chained=True)
    h2_s = _mix(h1, ys_s, v_s, gates_s, mod, SAMPLE_GROUP, npt, cache_conv[0].transpose(1, 0, 2), p, chained=False)

    ffn2 = (row(g_ffn2[0]), w1_ffn2[0].astype(BF16), w2_ffn2[0].astype(BF16), row(g_final))
    y_prompt = _ffn_out(h2_p, mod, modf, PROMPT_GROUP, *ffn2)
    y_sample = _ffn_out(h2_s, mod, modf, SAMPLE_GROUP, *ffn2).reshape(bs, ls, d)
    conv_p = v_p[rp - HIST:, s - 1, :].reshape(1, bp, HIST, d)
    conv_s = v_s[TILE_STEPS - HIST:].transpose(1, 0, 2)[None]
    st = lambda a, n: a[:n].reshape(1, n, N_GROUPS, SSM_STATE)
    return (y_prompt, y_sample, st(re_p, bp), st(im_p, bp), conv_p, st(re_s, bs), st(im_s, bs), conv_s)
```
